```python
import jax, jax.numpy as jnp
from jax import lax
import numpy as np

D_MODEL = 2048
BATCH = 8
SEQ = 4096
DEPTH = 2
DEC_BATCH = 8
DEC_SEQ = 16
PAST_LEN = 1024

CHUNK = 64
N_MIXERS = 2
H_A = 16
DH_A = D_MODEL // H_A
N_PREV_A = 8
REL_CLIP = 128
DH_B = 64
H_B = D_MODEL // DH_B
KV_B = H_B // 4
G_B = H_B // KV_B
WINDOW_B = 128
N_PREV_B = WINDOW_B // CHUNK
ROT_DIM = DH_B // 4
ROPE_THETA = 500000.0
D_FF = 4 * D_MODEL
ALPHA = (2 * DEPTH) ** 0.25
BETA = (8 * DEPTH) ** -0.25
LN_EPS = 1e-5
NEG_INF = -1e30

kernel_name = 'hybrid_streaming_encoder_step'


def layer_norm(x, g, b):
    xf = x.astype(jnp.float32)
    mu = xf.mean(-1, keepdims=True)
    var = jnp.square(xf - mu).mean(-1, keepdims=True)
    return ((xf - mu) * lax.rsqrt(var + LN_EPS)).astype(x.dtype) * g + b


def ada_modulation(c, w, b):
    mod = jax.nn.silu(c) @ w + b
    shift, scale, gate = jnp.split(mod, 3, axis=-1)
    return shift[:, None], scale[:, None], gate[:, None]


def pad_to_chunks(x):
    s = x.shape[1]
    s_pad = -(-s // CHUNK) * CHUNK
    return jnp.pad(x, ((0, 0), (0, s_pad - s)) + ((0, 0),) * (x.ndim - 2))


def extend_with_history(new, hist, n_prev):
    s = new.shape[1]
    s_pad = -(-s // CHUNK) * CHUNK
    hist_len = 0 if hist is None else hist.shape[1]
    body = new if hist is None else jnp.concatenate([hist, new], axis=1)
    front = n_prev * CHUNK - hist_len
    ext = jnp.pad(body, ((0, 0), (front, s_pad - s), (0, 0), (0, 0)))
    r = jnp.arange(ext.shape[1])
    valid = (r >= front) & (r < front + hist_len + s)
    return ext, valid


def rel_position_bias(table, n_prev):
    i = jnp.arange(CHUNK)[:, None]
    j = jnp.arange((n_prev + 1) * CHUNK)[None, :]
    idx = jnp.clip(n_prev * CHUNK + i - j, -REL_CLIP, REL_CLIP) + REL_CLIP
    return table[:, idx]


def band_attention(q, k_ext, v_ext, valid, n_prev, bias=None, sink=None):
    b, s_pad, hkv, g, dh = q.shape
    nc = s_pad // CHUNK
    band = (n_prev + 1) * CHUNK
    scale = dh ** -0.5
    qc = q.reshape(b, nc, CHUNK, hkv, g, dh).transpose(1, 0, 2, 3, 4, 5)

    def one_chunk(args):
        c, qb = args
        start = c * CHUNK
        kb = lax.dynamic_slice_in_dim(k_ext, start, band, axis=1)
        vb = lax.dynamic_slice_in_dim(v_ext, start, band, axis=1)
        ok = lax.dynamic_slice_in_dim(valid, start, band, axis=0)
        s = jnp.einsum('bqkgd,bjkd->bkgqj', qb, kb).astype(jnp.float32) * scale
        if bias is not None:
            s = s + bias.astype(jnp.float32)
        s = jnp.where(ok[None, None, None, None, :], s, NEG_INF)
        if sink is None:
            p = jax.nn.softmax(s, axis=-1)
        else:
            sk = sink.astype(jnp.float32)[None, :, :, None, None]
            m = jnp.maximum(s.max(-1, keepdims=True), sk)
            e = jnp.exp(s - m)
            p = e / (e.sum(-1, keepdims=True) + jnp.exp(sk - m))
        return jnp.einsum('bkgqj,bjkd->bqkgd', p.astype(vb.dtype), vb)

    out = lax.map(one_chunk, (jnp.arange(nc), qc))
    return out.transpose(1, 0, 2, 3, 4, 5).reshape(b, s_pad, hkv * g * dh)


def partial_rope(x, pos):
    half = ROT_DIM // 2
    freqs = ROPE_THETA ** (-jnp.arange(0, ROT_DIM, 2, dtype=jnp.float32) / ROT_DIM)
    ang = pos.astype(jnp.float32)[:, None] * freqs[None, :]
    cos = jnp.cos(ang)[None, :, None, :]
    sin = jnp.sin(ang)[None, :, None, :]
    xr = x[..., :ROT_DIM].astype(jnp.float32)
    x1, x2 = xr[..., :half], xr[..., half:]
    rot = jnp.concatenate([x1 * cos - x2 * sin, x2 * cos + x1 * sin], axis=-1).astype(x.dtype)
    return jnp.concatenate([rot, x[..., ROT_DIM:]], axis=-1)


def mixer_a(h, hist_k, hist_v, w_qkv, w_o, rel_table):
    b, s, _ = h.shape
    q, k, v = jnp.split(h @ w_qkv, 3, axis=-1)
    q = q.reshape(b, s, H_A, DH_A)
    k = k.reshape(b, s, H_A, DH_A)
    v = v.reshape(b, s, H_A, DH_A)
    k_ext, valid = extend_with_history(k, hist_k, N_PREV_A)
    v_ext, _ = extend_with_history(v, hist_v, N_PREV_A)
    bias = rel_position_bias(rel_table, N_PREV_A)[:, None]
    o = band_attention(pad_to_chunks(q)[:, :, :, None, :], k_ext, v_ext, valid, N_PREV_A, bias=bias)
    return o[:, :s] @ w_o, k, v


def mixer_b(h, pos0, hist_k, hist_v, w_qkv, w_o, sink):
    b, s, _ = h.shape
    q, k, v = jnp.split(h @ w_qkv, [H_B * DH_B, H_B * DH_B + KV_B * DH_B], axis=-1)
    pos = pos0 + jnp.arange(s)
    q = partial_rope(q.reshape(b, s, H_B, DH_B), pos)
    k = partial_rope(k.reshape(b, s, KV_B, DH_B), pos)
    v = v.reshape(b, s, KV_B, DH_B)
    k_ext, valid = extend_with_history(k, hist_k, N_PREV_B)
    v_ext, _ = extend_with_history(v, hist_v, N_PREV_B)
    qg = pad_to_chunks(q).reshape(b, -1, KV_B, G_B, DH_B)
    o = band_attention(qg, k_ext, v_ext, valid, N_PREV_B, sink=sink.reshape(KV_B, G_B))
    return o[:, :s] @ w_o, k, v


def trunk(x, c, pos0, hist_a_k, hist_a_v, hist_b_k, hist_b_v,
          w_ada, b_ada, ln_g, ln_b, w_qkv_a, w_o_a, rel_bias_a,
          w_qkv_b, w_o_b, sink_b, w_up, w_down):
    new_a_k, new_a_v, new_b_k, new_b_v = [], [], [], []
    for i in range(DEPTH):
        l = i // N_MIXERS
        shift, scale, gate = ada_modulation(c, w_ada[i, 0], b_ada[i, 0])
        h = x * (1 + scale) + shift
        if i % N_MIXERS == 0:
            hk = None if hist_a_k is None else hist_a_k[l]
            hv = None if hist_a_v is None else hist_a_v[l]
            y, k, v = mixer_a(h, hk, hv, w_qkv_a[l], w_o_a[l], rel_bias_a[l])
            new_a_k.append(k[:, -N_PREV_A * CHUNK:])
            new_a_v.append(v[:, -N_PREV_A * CHUNK:])
        else:
            hk = None if hist_b_k is None else hist_b_k[l]
            hv = None if hist_b_v is None else hist_b_v[l]
            y, k, v = mixer_b(h, pos0, hk, hv, w_qkv_b[l], w_o_b[l], sink_b[l])
            new_b_k.append(k[:, -N_PREV_B * CHUNK:])
            new_b_v.append(v[:, -N_PREV_B * CHUNK:])
        x = layer_norm(ALPHA * x + gate * y, ln_g[i, 0], ln_b[i, 0])
        shift, scale, gate = ada_modulation(c, w_ada[i, 1], b_ada[i, 1])
        h = x * (1 + scale) + shift
        f = jnp.square(jax.nn.relu(h @ w_up[i])) @ w_down[i]
        x = layer_norm(ALPHA * x + gate * f, ln_g[i, 1], ln_b[i, 1])
    return x, jnp.stack(new_a_k), jnp.stack(new_a_v), jnp.stack(new_b_k), jnp.stack(new_b_v)


def setup_inputs(seed: int = 0) -> dict:
    key = jax.random.key(seed)
    ks = jax.random.split(key, 20)
    n_a = (DEPTH + 1) // 2
    n_b = DEPTH // 2
    la = min(N_PREV_A * CHUNK, PAST_LEN)
    lb = min(N_PREV_B * CHUNK, PAST_LEN)
    d = D_MODEL

    def nrm(k, shape, s):
        return jax.random.normal(k, shape, jnp.float32) * s

    return {
        'x_prompt': nrm(ks[0], (BATCH, SEQ, d), 1.0),
        'x_sample': nrm(ks[1], (DEC_BATCH, DEC_SEQ, d), 1.0),
        'cache_a_k': nrm(ks[2], (n_a, DEC_BATCH, la, H_A, DH_A), 1.0),
        'cache_a_v': nrm(ks[3], (n_a, DEC_BATCH, la, H_A, DH_A), 1.0),
        'cache_b_k': nrm(ks[4], (n_b, DEC_BATCH, lb, KV_B, DH_B), 1.0),
        'cache_b_v': nrm(ks[5], (n_b, DEC_BATCH, lb, KV_B, DH_B), 1.0),
        'c_prompt': nrm(ks[6], (BATCH, d), 1.0),
        'c_sample': nrm(ks[7], (DEC_BATCH, d), 1.0),
        'w_ada': nrm(ks[8], (DEPTH, 2, d, 3 * d), 0.5 * d ** -0.5),
        'b_ada': nrm(ks[9], (DEPTH, 2, 3 * d), 0.02),
        'ln_g': 1.0 + nrm(ks[10], (DEPTH, 2, d), 0.05),
        'ln_b': nrm(ks[11], (DEPTH, 2, d), 0.02),
        'w_qkv_a': nrm(ks[12], (n_a, d, 3 * H_A * DH_A), d ** -0.5),
        'w_o_a': nrm(ks[13], (n_a, H_A * DH_A, d), BETA * (H_A * DH_A) ** -0.5),
        'rel_bias_a': nrm(ks[14], (n_a, H_A, 2 * REL_CLIP + 1), 0.5),
        'w_qkv_b': nrm(ks[15], (n_b, d, H_B * DH_B + 2 * KV_B * DH_B), d ** -0.5),
        'w_o_b': nrm(ks[16], (n_b, H_B * DH_B, d), BETA * (H_B * DH_B) ** -0.5),
        'sink_b': nrm(ks[17], (n_b, H_B), 0.5),
        'w_up': nrm(ks[18], (DEPTH, d, D_FF), d ** -0.5),
        'w_down': nrm(ks[19], (DEPTH, D_FF, d), BETA * D_FF ** -0.5),
    }


def reference(x_prompt, x_sample, cache_a_k, cache_a_v, cache_b_k, cache_b_v, c_prompt, c_sample,
              w_ada, b_ada, ln_g, ln_b, w_qkv_a, w_o_a, rel_bias_a, w_qkv_b, w_o_b, sink_b, w_up, w_down):
    y_prompt, ak_p, av_p, bk_p, bv_p = trunk(
        x_prompt, c_prompt, 0, None, None, None, None,
        w_ada, b_ada, ln_g, ln_b, w_qkv_a, w_o_a, rel_bias_a, w_qkv_b, w_o_b, sink_b, w_up, w_down)
    y_sample, ak_s, av_s, bk_s, bv_s = trunk(
        x_sample, c_sample, PAST_LEN, cache_a_k, cache_a_v, cache_b_k, cache_b_v,
        w_ada, b_ada, ln_g, ln_b, w_qkv_a, w_o_a, rel_bias_a, w_qkv_b, w_o_b, sink_b, w_up, w_down)
    return (y_prompt, y_sample, ak_p, av_p, bk_p, bv_p, ak_s, av_s, bk_s, bv_s)
```

```python
import functools

import numpy as np
import jax
import jax.numpy as jnp
from jax import lax
from jax.experimental import pallas as pl
from jax.experimental.pallas import tpu as pltpu

F32 = jnp.float32
BF16 = jnp.bfloat16

D_MODEL = 2048
DEPTH = 2
PAST_LEN = 1024
CHUNK = 64
CHUNK_SHIFT = 6
H_A = 16
DH_A = D_MODEL // H_A
N_PREV_A = 8
REL_CLIP = 128
DH_B = 64
H_B = D_MODEL // DH_B
KV_B = H_B // 4
G_B = H_B // KV_B
N_PREV_B = 2
ROT_DIM = DH_B // 4
ROPE_THETA = 500000.0
D_FF = 4 * D_MODEL
ALPHA = (2 * DEPTH) ** 0.25
LN_EPS = 1e-5
NEG_INF = -1e30

LANES = 128
MIB = 1024 * 1024
VMEM_CAP_BYTES = 56 * MIB
FFN_DOWN_COLS = 512


def _params(semantics, vmem_bytes):
    return pltpu.CompilerParams(
        dimension_semantics=semantics,
        vmem_limit_bytes=int(min(VMEM_CAP_BYTES, vmem_bytes)))


def _layer_norm(z, g, b):
    mu = jnp.mean(z, axis=-1, keepdims=True)
    zc = z - mu
    var = jnp.mean(zc * zc, axis=-1, keepdims=True)
    return zc * lax.rsqrt(var + LN_EPS) * g + b


def _ada_kernel(c_ref, w_ref, b_ref, o_ref):
    c = c_ref[...]
    sc = (c * (1.0 / (1.0 + jnp.exp(-c)))).astype(BF16)
    o_ref[0] = jnp.dot(sc, w_ref[0].astype(BF16), preferred_element_type=F32) + b_ref[0]


def _ada_call(c_all, w4, b4, *, tn=1024):
    n_pairs, d, n = w4.shape
    rows = c_all.shape[0]
    return pl.pallas_call(
        _ada_kernel,
        grid=(n_pairs, n // tn),
        in_specs=[
            pl.BlockSpec((rows, d), lambda p, j: (0, 0)),
            pl.BlockSpec((1, d, tn), lambda p, j: (p, 0, j)),
            pl.BlockSpec((1, 1, tn), lambda p, j: (p, 0, j)),
        ],
        out_specs=pl.BlockSpec((1, rows, tn), lambda p, j: (p, 0, j)),
        out_shape=jax.ShapeDtypeStruct((n_pairs, rows, n), F32),
        compiler_params=_params(("arbitrary", "arbitrary"), 2 * d * tn * 4 + 16 * MIB),
        name="ada_mod",
    )(c_all, w4, b4)


def _rope(acc, cos_t, sin_up, sin_dn):
    half = ROT_DIM // 2
    parts = []
    for c in range(acc.shape[1] // LANES):
        a = acc[:, c * LANES:(c + 1) * LANES]
        parts.append(a * cos_t
                     + pltpu.roll(a, LANES - half, 1) * sin_up
                     + pltpu.roll(a, half, 1) * sin_dn)
    return jnp.concatenate(parts, axis=1)


def _qkv_kernel(*refs, nq, nkv, rope):
    if rope:
        x_ref, shift_ref, scale_ref, w_ref, cos_ref, sup_ref, sdn_ref, q_ref, kv_ref, h_scr = refs
    else:
        x_ref, shift_ref, scale_ref, w_ref, q_ref, kv_ref, h_scr = refs
    j = pl.program_id(1)

    @pl.when(j == 0)
    def _():
        h = x_ref[...] * (1.0 + scale_ref[0]) + shift_ref[0]
        h_scr[...] = h.astype(BF16)

    acc = jnp.dot(h_scr[...], w_ref[...], preferred_element_type=F32)

    if rope:
        @pl.when(j < nq)
        def _():
            q_ref[...] = _rope(acc, cos_ref[...], sup_ref[...], sdn_ref[...]).astype(BF16)

        @pl.when(jnp.logical_and(j >= nq, j < nq + nkv))
        def _():
            kv_ref[0] = _rope(acc, cos_ref[...], sup_ref[...], sdn_ref[...])

        @pl.when(j >= nq + nkv)
        def _():
            kv_ref[0] = acc
    else:
        @pl.when(j < nq)
        def _():
            q_ref[...] = acc.astype(BF16)

        @pl.when(j >= nq)
        def _():
            kv_ref[0] = acc


def _qkv_call(x2d, shift, scale, w, *, q_cols, kv_cols, tm, tn, tiles_per_group,
              rope_tabs=None, name):
    t, d = x2d.shape
    nq, nkv = q_cols // tn, kv_cols // tn
    r = shift.shape[1]
    rope = rope_tabs is not None
    mod_spec = pl.BlockSpec((1, r, d), lambda i, j: (i // tiles_per_group, 0, 0))
    in_specs = [pl.BlockSpec((tm, d), lambda i, j: (i, 0)), mod_spec, mod_spec,
                pl.BlockSpec((d, tn), lambda i, j: (0, j))]
    args = [x2d, shift, scale, w]
    if rope:
        n_tab = rope_tabs[0].shape[0] // tm
        tab_spec = pl.BlockSpec((tm, LANES), lambda i, j: (i % n_tab, 0))
        in_specs += [tab_spec] * 3
        args += list(rope_tabs)

    def kv_map(i, j):
        jj = jnp.maximum(j - nq, 0)
        return (jj // nkv, i, jj % nkv)

    vmem = 2 * tm * d * 4 + tm * d * 2 + 2 * d * tn * 2 + 2 * tm * tn * 6 + 4 * tm * tn * 4 + 8 * MIB
    return pl.pallas_call(
        functools.partial(_qkv_kernel, nq=nq, nkv=nkv, rope=rope),
        grid=(t // tm, nq + 2 * nkv),
        in_specs=in_specs,
        out_specs=[pl.BlockSpec((tm, tn), lambda i, j: (i, jnp.minimum(j, nq - 1))),
                   pl.BlockSpec((1, tm, tn), kv_map)],
        out_shape=[jax.ShapeDtypeStruct((t, q_cols), BF16),
                   jax.ShapeDtypeStruct((2, t, kv_cols), F32)],
        scratch_shapes=[pltpu.VMEM((tm, d), BF16)],
        compiler_params=_params(("arbitrary", "arbitrary"), vmem),
        name=name,
    )(*args)


def _attn_a_kernel(q_ref, k_ref, v_ref, g_ref, o_ref, *, tq, w, nqb, q_offset, nvar):
    scale = DH_A ** -0.5

    def block(qi):
        if isinstance(qi, int):
            qstart = qi * tq
            kstart = max(qstart + q_offset - N_PREV_A * CHUNK, 0)
            var = min(qi, nvar - 1)
        else:
            qstart = pl.multiple_of(qi * tq, tq)
            kstart = pl.multiple_of(
                jnp.maximum(qstart + (q_offset - N_PREV_A * CHUNK), 0), CHUNK)
            var = jnp.minimum(qi, nvar - 1)
        q = q_ref[0, pl.ds(qstart, tq), :]
        kw = k_ref[0, 0, pl.ds(kstart, w), :].astype(BF16)
        vw = v_ref[0, 0, pl.ds(kstart, w), :].astype(BF16)
        s = lax.dot_general(q, kw, (((1,), (1,)), ((), ())), preferred_element_type=F32)
        s = s * scale + g_ref[0, var]
        m = jnp.max(s, axis=-1, keepdims=True)
        e = jnp.exp(s - m)
        l = jnp.sum(e, axis=-1, keepdims=True)
        o = jnp.dot(e.astype(BF16), vw, preferred_element_type=F32)
        o_ref[0, pl.ds(qstart, tq), :] = (o * (1.0 / l)).astype(BF16)

    if nqb == 1:
        block(0)
    else:
        def body(qi, carry):
            block(qi)
            return carry
        lax.fori_loop(0, nqb, body, 0)


def _bias_tiles_a(rel_table, *, tq, w, q_offset, nvar, n_valid):
    band = N_PREV_A * CHUNK
    i = np.arange(tq)[:, None]
    j = np.arange(w)[None, :]
    tiles = []
    for v in range(nvar):
        qoff = min(q_offset + v * tq, band)
        idx = np.clip(qoff + i - j, -REL_CLIP, REL_CLIP) + REL_CLIP
        cq = (qoff + i) // CHUNK
        ck = j // CHUNK
        valid = (ck <= cq) & (ck >= cq - N_PREV_A) & (j < n_valid)
        tiles.append(jnp.where(valid[None], rel_table[:, idx], NEG_INF))
    return jnp.stack(tiles, axis=1).astype(F32)


def _attn_a_call(q, kv, rel_table, *, tq, q_offset, n_valid, name):
    b, sq, _ = q.shape
    sk = kv.shape[2]
    band = N_PREV_A * CHUNK
    w = band + tq
    nqb = sq // tq
    nvar = -(-(band - q_offset) // tq) + 1
    assert nqb == 1 or n_valid == sk
    assert (nqb - 1) * tq + q_offset - band + w <= sk
    g = _bias_tiles_a(rel_table, tq=tq, w=w, q_offset=q_offset, nvar=nvar, n_valid=n_valid)
    vmem = 2 * (2 * sq * DH_A * 2 + 2 * sk * DH_A * 4 + nvar * tq * w * 4) + 10 * tq * w * 4 + 4 * MIB
    return pl.pallas_call(
        functools.partial(_attn_a_kernel, tq=tq, w=w, nqb=nqb, q_offset=q_offset, nvar=nvar),
        grid=(H_A, b),
        in_specs=[
            pl.BlockSpec((1, sq, DH_A), lambda h, bi: (bi, 0, h)),
            pl.BlockSpec((1, 1, sk, DH_A), lambda h, bi: (0, bi, 0, h)),
            pl.BlockSpec((1, 1, sk, DH_A), lambda h, bi: (1, bi, 0, h)),
            pl.BlockSpec((1, nvar, tq, w), lambda h, bi: (h, 0, 0, 0)),
        ],
        out_specs=pl.BlockSpec((1, sq, DH_A), lambda h, bi: (bi, 0, h)),
        out_shape=jax.ShapeDtypeStruct(q.shape, BF16),
        compiler_params=_params(("arbitrary", "arbitrary"), vmem),
        name=name,
    )(q, kv, kv, g)


def _attn_b_kernel(sink_ref, q_ref, k_ref, v_ref, o_ref, *, tq, w, nqb, q_offset, n_valid):
    p = pl.program_id(0)
    scale = DH_B ** -0.5
    lane_even = lax.broadcasted_iota(jnp.int32, (tq, LANES), 1) < DH_B
    row = lax.broadcasted_iota(jnp.int32, (2 * tq, 1), 0)
    row_even = row < tq
    qrow = jnp.where(row_even, row, row - tq)
    kcol = lax.broadcasted_iota(jnp.int32, (1, w), 1)

    def block(qi):
        if isinstance(qi, int):
            qstart = qi * tq
            kstart = max(qstart + q_offset - N_PREV_B * CHUNK, 0)
        else:
            qstart = pl.multiple_of(qi * tq, tq)
            kstart = pl.multiple_of(
                jnp.maximum(qstart + (q_offset - N_PREV_B * CHUNK), 0), CHUNK)
        qoff = qstart + q_offset - kstart
        kw = k_ref[0, 0, pl.ds(kstart, w), :].astype(BF16)
        vw = v_ref[0, 0, pl.ds(kstart, w), :].astype(BF16)
        cq = (qoff + qrow) >> CHUNK_SHIFT
        ck = kcol >> CHUNK_SHIFT
        valid = (ck <= cq) & (ck >= cq - N_PREV_B) & (kcol + kstart < n_valid)
        for t in range(G_B):
            qt = q_ref[0, pl.ds(qstart, tq), t * LANES:(t + 1) * LANES]
            zero = jnp.zeros_like(qt)
            lhs = jnp.concatenate([jnp.where(lane_even, qt, zero),
                                   jnp.where(lane_even, zero, qt)], axis=0)
            s = lax.dot_general(lhs, kw, (((1,), (1,)), ((), ())), preferred_element_type=F32)
            s = jnp.where(valid, s * scale, NEG_INF)
            sk = jnp.where(row_even, sink_ref[2 * G_B * p + t], sink_ref[2 * G_B * p + G_B + t])
            m = jnp.maximum(jnp.max(s, axis=-1, keepdims=True), sk)
            e = jnp.exp(s - m)
            den = jnp.sum(e, axis=-1, keepdims=True) + jnp.exp(sk - m)
            o2 = jnp.dot(e.astype(BF16), vw, preferred_element_type=F32) * (1.0 / den)
            o = jnp.where(lane_even, o2[:tq], o2[tq:])
            o_ref[0, pl.ds(qstart, tq), t * LANES:(t + 1) * LANES] = o.astype(BF16)

    if nqb == 1:
        block(0)
    else:
        def body(qi, carry):
            block(qi)
            return carry
        lax.fori_loop(0, nqb, body, 0)


def _attn_b_call(q, kv, sink, *, tq, q_offset, n_valid, name):
    b, sq, _ = q.shape
    sk = kv.shape[2]
    w = N_PREV_B * CHUNK + tq
    nqb = sq // tq
    n_pairs = KV_B // 2
    qw = G_B * LANES
    assert (nqb - 1) * tq + q_offset - N_PREV_B * CHUNK + w <= sk
    vmem = 2 * (2 * sq * qw * 2 + 2 * sk * LANES * 4) + 24 * tq * w * 4 + 4 * MIB
    return pl.pallas_call(
        functools.partial(_attn_b_kernel, tq=tq, w=w, nqb=nqb, q_offset=q_offset, n_valid=n_valid),
        grid=(n_pairs, b),
        in_specs=[
            pl.BlockSpec(memory_space=pltpu.SMEM),
            pl.BlockSpec((1, sq, qw), lambda p, bi: (bi, 0, p)),
            pl.BlockSpec((1, 1, sk, LANES), lambda p, bi: (0, bi, 0, p)),
            pl.BlockSpec((1, 1, sk, LANES), lambda p, bi: (1, bi, 0, p)),
        ],
        out_specs=pl.BlockSpec((1, sq, qw), lambda p, bi: (bi, 0, p)),
        out_shape=jax.ShapeDtypeStruct(q.shape, BF16),
        compiler_params=_params(("arbitrary", "arbitrary"), vmem),
        name=name,
    )(sink, q, kv, kv)


def _wo_ln_kernel(o_ref, x_ref, gate_ref, w_ref, g_ref, b_ref, out_ref):
    y = jnp.dot(o_ref[...], w_ref[...], preferred_element_type=F32)
    z = ALPHA * x_ref[...] + gate_ref[0] * y
    out_ref[...] = _layer_norm(z, g_ref[...], b_ref[...])


def _wo_ln_call(o2d, x2d, gate, w, ln_g, ln_b, *, tm, tiles_per_group, name):
    t, d = x2d.shape
    r = gate.shape[1]
    vmem = 2 * (tm * d * 2 + 2 * tm * d * 4 + d * d * 2) + 4 * tm * d * 4 + 4 * MIB
    return pl.pallas_call(
        _wo_ln_kernel,
        grid=(t // tm,),
        in_specs=[
            pl.BlockSpec((tm, d), lambda i: (i, 0)),
            pl.BlockSpec((tm, d), lambda i: (i, 0)),
            pl.BlockSpec((1, r, d), lambda i: (i // tiles_per_group, 0, 0)),
            pl.BlockSpec((d, d), lambda i: (0, 0)),
            pl.BlockSpec((1, d), lambda i: (0, 0)),
            pl.BlockSpec((1, d), lambda i: (0, 0)),
        ],
        out_specs=pl.BlockSpec((tm, d), lambda i: (i, 0)),
        out_shape=jax.ShapeDtypeStruct((t, d), F32),
        compiler_params=_params(("arbitrary",), vmem),
        name=name,
    )(o2d, x2d, gate, w, ln_g, ln_b)


def _ffn_kernel(x_ref, shift_ref, scale_ref, gate_ref, wu_ref, wd_ref, g_ref, b_ref,
                out_ref, h_scr):
    j = pl.program_id(1)

    @pl.when(j == 0)
    def _():
        h = x_ref[...] * (1.0 + scale_ref[0]) + shift_ref[0]
        h_scr[...] = h.astype(BF16)
        out_ref[...] = jnp.zeros_like(out_ref)

    u = jnp.maximum(jnp.dot(h_scr[...], wu_ref[...], preferred_element_type=F32), 0.0)
    ub = (u * u).astype(BF16)
    d = out_ref.shape[1]
    for c in range(d // FFN_DOWN_COLS):
        cols = slice(c * FFN_DOWN_COLS, (c + 1) * FFN_DOWN_COLS)
        out_ref[:, cols] += jnp.dot(ub, wd_ref[:, cols], preferred_element_type=F32)

    @pl.when(j == pl.num_programs(1) - 1)
    def _():
        z = ALPHA * x_ref[...] + gate_ref[0] * out_ref[...]
        out_ref[...] = _layer_norm(z, g_ref[...], b_ref[...])


def _ffn_call(x2d, shift, scale, gate, w_up, w_down, ln_g, ln_b, *, tm, tf, tiles_per_group, name):
    t, d = x2d.shape
    dff = w_up.shape[1]
    r = gate.shape[1]
    mod_spec = pl.BlockSpec((1, r, d), lambda i, j: (i // tiles_per_group, 0, 0))
    vmem = 4 * tm * d * 4 + tm * d * 2 + 4 * d * tf * 2 + 3 * tm * tf * 4 + 2 * tm * d * 4 + 4 * MIB
    return pl.pallas_call(
        _ffn_kernel,
        grid=(t // tm, dff // tf),
        in_specs=[
            pl.BlockSpec((tm, d), lambda i, j: (i, 0)),
            mod_spec, mod_spec, mod_spec,
            pl.BlockSpec((d, tf), lambda i, j: (0, j)),
            pl.BlockSpec((tf, d), lambda i, j: (j, 0)),
            pl.BlockSpec((1, d), lambda i, j: (0, 0)),
            pl.BlockSpec((1, d), lambda i, j: (0, 0)),
        ],
        out_specs=pl.BlockSpec((tm, d), lambda i, j: (i, 0)),
        out_shape=jax.ShapeDtypeStruct((t, d), F32),
        scratch_shapes=[pltpu.VMEM((tm, d), BF16)],
        compiler_params=_params(("arbitrary", "arbitrary"), vmem),
        name=name,
    )(x2d, shift, scale, gate, w_up, w_down, ln_g, ln_b)


def _rope_tables(pos):
    half = ROT_DIM // 2
    freqs = ROPE_THETA ** (-jnp.arange(0, ROT_DIM, 2, dtype=F32) / ROT_DIM)
    ang = pos.astype(F32)[:, None] * freqs[None, :]
    cos, sin = jnp.cos(ang), jnp.sin(ang)
    n = pos.shape[0]
    ones = jnp.ones((n, DH_B - ROT_DIM), F32)
    zeros = jnp.zeros((n, DH_B - ROT_DIM), F32)
    z8 = jnp.zeros((n, half), F32)
    cos_h = jnp.concatenate([cos, cos, ones], axis=1)
    sup_h = jnp.concatenate([-sin, z8, zeros], axis=1)
    sdn_h = jnp.concatenate([z8, sin, zeros], axis=1)
    return tuple(jnp.concatenate([t, t], axis=1) for t in (cos_h, sup_h, sdn_h))


def _paired_head_order():
    order = []
    for p in range(KV_B // 2):
        for t in range(G_B):
            order += [2 * G_B * p + t, 2 * G_B * p + G_B + t]
    return np.asarray(order)


def _trunk(x, mod, pos0, hist, wts, *, tm, tf, tq_a, tq_b):
    bn, s, d = x.shape
    t = bn * s
    per_row = tm > s
    tm_wo = min(tm, 512)
    if per_row:
        assert t == tm
        tiles_per_group = 1

        def mod3(m):
            return jnp.repeat(m, s, axis=0).reshape(1, t, d)
    else:
        tiles_per_group = s // tm

        def mod3(m):
            return m.reshape(bn, 1, d)

    def split_mod(p):
        m = mod[p]
        return mod3(m[:, :d]), mod3(m[:, d:2 * d]), mod3(m[:, 2 * d:])

    x2d = x.reshape(t, d)
    pos = pos0 + jnp.arange(s)
    rope_tabs = _rope_tables(pos)
    if per_row:
        rope_tabs = tuple(jnp.tile(tb, (bn, 1)) for tb in rope_tabs)

    def pad_rows(a, n):
        return jnp.pad(a, ((0, 0), (0, n - a.shape[1]), (0, 0)))

    shift, scale, gate = split_mod(0)
    q, kv = _qkv_call(x2d, shift, scale, wts["w_qkv_a"], q_cols=d, kv_cols=d, tm=tm,
                      tn=min(1024, d), tiles_per_group=tiles_per_group, name="qkv_a")
    new_a = kv.reshape(2, bn, s, H_A, DH_A)[:, None, :, -N_PREV_A * CHUNK:]
    q = q.reshape(bn, s, d)
    kv = kv.reshape(2, bn, s, d)
    if hist is None:
        o = _attn_a_call(q, kv, wts["rel_bias_a"], tq=tq_a, q_offset=0, n_valid=s, name="attn_a")
    else:
        hk = hist["a_k"].reshape(bn, -1, d)
        hv = hist["a_v"].reshape(bn, -1, d)
        hl = hk.shape[1]
        sk = N_PREV_A * CHUNK + tq_a
        kv_ext = jnp.stack([pad_rows(jnp.concatenate([hk, kv[0]], axis=1), sk),
                            pad_rows(jnp.concatenate([hv, kv[1]], axis=1), sk)])
        o = _attn_a_call(pad_rows(q, tq_a), kv_ext, wts["rel_bias_a"], tq=tq_a,
                         q_offset=hl, n_valid=hl + s, name="attn_a")[:, :s]
    x2d = _wo_ln_call(o.reshape(t, d), x2d, gate, wts["w_o_a"], wts["ln_g"][0, 0], wts["ln_b"][0, 0],
                      tm=tm_wo, tiles_per_group=tiles_per_group * (tm // tm_wo), name="wo_ln_a")
    shift, scale, gate = split_mod(1)
    x2d = _ffn_call(x2d, shift, scale, gate, wts["w_up"][0], wts["w_down"][0],
                    wts["ln_g"][0, 1], wts["ln_b"][0, 1], tm=tm, tf=tf,
                    tiles_per_group=tiles_per_group, name="ffn_0")

    shift, scale, gate = split_mod(2)
    kvw = KV_B * DH_B
    q, kv = _qkv_call(x2d, shift, scale, wts["w_qkv_b"], q_cols=d, kv_cols=kvw, tm=tm,
                      tn=kvw, tiles_per_group=tiles_per_group, rope_tabs=rope_tabs, name="qkv_b")
    new_b = kv.reshape(2, bn, s, KV_B, DH_B)[:, None, :, -N_PREV_B * CHUNK:]
    q = q.reshape(bn, s, d)
    kv = kv.reshape(2, bn, s, kvw)
    if hist is None:
        o = _attn_b_call(q, kv, wts["sink_b"], tq=tq_b, q_offset=0, n_valid=s, name="attn_b")
    else:
        hk = hist["b_k"].reshape(bn, -1, kvw)
        hv = hist["b_v"].reshape(bn, -1, kvw)
        hl = hk.shape[1]
        sk = N_PREV_B * CHUNK + tq_b
        kv_ext = jnp.stack([pad_rows(jnp.concatenate([hk, kv[0]], axis=1), sk),
                            pad_rows(jnp.concatenate([hv, kv[1]], axis=1), sk)])
        o = _attn_b_call(pad_rows(q, tq_b), kv_ext, wts["sink_b"], tq=tq_b,
                         q_offset=hl, n_valid=hl + s, name="attn_b")[:, :s]
    x2d = _wo_ln_call(o.reshape(t, d), x2d, gate, wts["w_o_b"], wts["ln_g"][1, 0], wts["ln_b"][1, 0],
                      tm=tm_wo, tiles_per_group=tiles_per_group * (tm // tm_wo), name="wo_ln_b")
    shift, scale, gate = split_mod(3)
    x2d = _ffn_call(x2d, shift, scale, gate, wts["w_up"][1], wts["w_down"][1],
                    wts["ln_g"][1, 1], wts["ln_b"][1, 1], tm=tm, tf=tf,
                    tiles_per_group=tiles_per_group, name="ffn_1")
    return x2d.reshape(bn, s, d), new_a[0], new_a[1], new_b[0], new_b[1]


def kernel(x_prompt, x_sample, cache_a_k, cache_a_v, cache_b_k, cache_b_v, c_prompt, c_sample,
           w_ada, b_ada, ln_g, ln_b, w_qkv_a, w_o_a, rel_bias_a, w_qkv_b, w_o_b, sink_b, w_up, w_down):
    d = D_MODEL
    bp = c_prompt.shape[0]
    c_all = jnp.concatenate([c_prompt, c_sample], axis=0)
    mod = _ada_call(c_all, w_ada.reshape(DEPTH * 2, d, 3 * d), b_ada.reshape(DEPTH * 2, 1, 3 * d))

    order = _paired_head_order()
    wq_b = w_qkv_b[0, :, :d].reshape(d, H_B, DH_B)[:, order].reshape(d, d)
    wts = {
        "w_qkv_a": w_qkv_a[0].astype(BF16),
        "w_o_a": w_o_a[0].astype(BF16),
        "w_qkv_b": jnp.concatenate([wq_b, w_qkv_b[0, :, d:]], axis=1).astype(BF16),
        "w_o_b": w_o_b[0].reshape(H_B, DH_B, d)[order].reshape(d, d).astype(BF16),
        "w_up": w_up.astype(BF16),
        "w_down": w_down.astype(BF16),
        "rel_bias_a": rel_bias_a[0],
        "sink_b": sink_b[0],
        "ln_g": ln_g.reshape(DEPTH, 2, 1, d),
        "ln_b": ln_b.reshape(DEPTH, 2, 1, d),
    }

    y_p, ak_p, av_p, bk_p, bv_p = _trunk(
        x_prompt, mod[:, :bp], 0, None, wts, tm=1024, tf=512, tq_a=256, tq_b=128)
    hist = {"a_k": cache_a_k[0], "a_v": cache_a_v[0], "b_k": cache_b_k[0], "b_v": cache_b_v[0]}
    n_s = x_sample.shape[0] * x_sample.shape[1]
    y_s, ak_s, av_s, bk_s, bv_s = _trunk(
        x_sample, mod[:, bp:], PAST_LEN, hist, wts, tm=n_s, tf=512, tq_a=CHUNK, tq_b=2 * CHUNK)
    return (y_p, y_s, ak_p, av_p, bk_p, bv_p, ak_s, av_s, bk_s, bv_s)
```

```python
import functools
import math

import numpy as np
import jax
import jax.numpy as jnp
from jax import lax
from jax.experimental import pallas as pl
from jax.experimental.pallas import tpu as pltpu

F32 = jnp.float32
BF16 = jnp.bfloat16

D_MODEL = 2048
DEPTH = 2
PAST_LEN = 1024
CHUNK = 64
H_A = 16
DH_A = D_MODEL // H_A
N_PREV_A = 8
REL_CLIP = 128
DH_B = 64
H_B = D_MODEL // DH_B
KV_B = H_B // 4
G_B = H_B // KV_B
N_PREV_B = 2
ROT_DIM = DH_B // 4
ROPE_THETA = 500000.0
D_FF = 4 * D_MODEL
ALPHA = (2 * DEPTH) ** 0.25
LN_EPS = 1e-5
NEG_INF = -1e30
LOG2E = math.log2(math.e)

LANES = 128
MXU_COLS = 256
MIB = 1024 * 1024
VMEM_CAP_BYTES = 56 * MIB
FFN_DOWN_COLS = 512
CAST_ROWS = 512


def _params(semantics, vmem_bytes):
    return pltpu.CompilerParams(
        dimension_semantics=semantics,
        vmem_limit_bytes=int(min(VMEM_CAP_BYTES, vmem_bytes)))


def _layer_norm(z, g, b):
    mu = jnp.mean(z, axis=-1, keepdims=True)
    zc = z - mu
    var = jnp.mean(zc * zc, axis=-1, keepdims=True)
    return zc * lax.rsqrt(var + LN_EPS) * g + b


def _ada_kernel(c_ref, w_ref, b_ref, o_ref):
    c = c_ref[...]
    sc = (c * (1.0 / (1.0 + jnp.exp(-c)))).astype(BF16)
    o_ref[0] = jnp.dot(sc, w_ref[0].astype(BF16), preferred_element_type=F32) + b_ref[0]


def _ada_call(c_all, w4, b4, *, tn=1024):
    n_pairs, d, n = w4.shape
    rows = c_all.shape[0]
    return pl.pallas_call(
        _ada_kernel,
        grid=(n_pairs, n // tn),
        in_specs=[
            pl.BlockSpec((rows, d), lambda p, j: (0, 0)),
            pl.BlockSpec((1, d, tn), lambda p, j: (p, 0, j)),
            pl.BlockSpec((1, 1, tn), lambda p, j: (p, 0, j)),
        ],
        out_specs=pl.BlockSpec((1, rows, tn), lambda p, j: (p, 0, j)),
        out_shape=jax.ShapeDtypeStruct((n_pairs, rows, n), F32),
        compiler_params=_params(("arbitrary", "arbitrary"), 2 * d * tn * 4 + 16 * MIB),
        name="ada_mod",
    )(c_all, w4, b4)


def _rope(acc, cos_t, sin_up, sin_dn):
    half = ROT_DIM // 2
    parts = []
    for c in range(acc.shape[1] // LANES):
        a = acc[:, c * LANES:(c + 1) * LANES]
        parts.append(a * cos_t
                     + pltpu.roll(a, LANES - half, 1) * sin_up
                     + pltpu.roll(a, half, 1) * sin_dn)
    return jnp.concatenate(parts, axis=1)


def _qkv_kernel(x_ref, shift_ref, scale_ref, w_ref, q_ref, kv_ref, h_scr, *, nq, q_scale):
    j = pl.program_id(1)
    tn = w_ref.shape[1]
    cw = min(MXU_COLS, tn)

    @pl.when(j == 0)
    def _():
        h = x_ref[...] * (1.0 + scale_ref[0]) + shift_ref[0]
        h_scr[...] = h.astype(BF16)

    def for_chunks(store):
        for c in range(tn // cw):
            cols = slice(c * cw, (c + 1) * cw)
            store(cols, jnp.dot(h_scr[...], w_ref[:, cols], preferred_element_type=F32))

    @pl.when(j < nq)
    def _():
        def store(cols, acc):
            q_ref[:, cols] = (acc * q_scale).astype(BF16)
        for_chunks(store)

    @pl.when(j >= nq)
    def _():
        def store(cols, acc):
            kv_ref[0, :, cols] = acc
        for_chunks(store)


def _qkv_call(x2d, shift, scale, w, *, q_cols, kv_cols, q_scale, tm, tn, tiles_per_group, name):
    t, d = x2d.shape
    nq, nkv = q_cols // tn, kv_cols // tn
    r = shift.shape[1]
    mod_spec = pl.BlockSpec((1, r, d), lambda i, j: (i // tiles_per_group, 0, 0))

    def kv_map(i, j):
        jj = jnp.maximum(j - nq, 0)
        return (jj // nkv, i, jj % nkv)

    vmem = 2 * tm * d * 4 + tm * d * 2 + 2 * d * tn * 2 + 2 * tm * tn * 6 + 8 * tm * MXU_COLS * 4 + 8 * MIB
    return pl.pallas_call(
        functools.partial(_qkv_kernel, nq=nq, q_scale=q_scale),
        grid=(t // tm, nq + 2 * nkv),
        in_specs=[pl.BlockSpec((tm, d), lambda i, j: (i, 0)), mod_spec, mod_spec,
                  pl.BlockSpec((d, tn), lambda i, j: (0, j))],
        out_specs=[pl.BlockSpec((tm, tn), lambda i, j: (i, jnp.minimum(j, nq - 1))),
                   pl.BlockSpec((1, tm, tn), kv_map)],
        out_shape=[jax.ShapeDtypeStruct((t, q_cols), BF16),
                   jax.ShapeDtypeStruct((2, t, kv_cols), F32)],
        scratch_shapes=[pltpu.VMEM((tm, d), BF16)],
        compiler_params=_params(("arbitrary", "arbitrary"), vmem),
        name=name,
    )(x2d, shift, scale, w)


def _qkv_rope_kernel(x_ref, shift_ref, scale_ref, w_ref, cos_ref, sup_ref, sdn_ref, q_ref, kv_ref,
                     h_scr, *, q_scale):
    h = x_ref[...] * (1.0 + scale_ref[0]) + shift_ref[0]
    h_scr[...] = h.astype(BF16)
    q_cols, kv_cols = q_ref.shape[1], kv_ref.shape[2]
    cw = MXU_COLS
    for c0 in range(0, q_cols + 2 * kv_cols, cw):
        acc = jnp.dot(h_scr[...], w_ref[:, c0:c0 + cw], preferred_element_type=F32)
        if c0 < q_cols:
            rot = _rope(acc, cos_ref[...], sup_ref[...], sdn_ref[...])
            q_ref[:, c0:c0 + cw] = (rot * q_scale).astype(BF16)
        elif c0 < q_cols + kv_cols:
            k0 = c0 - q_cols
            kv_ref[0, :, k0:k0 + cw] = _rope(acc, cos_ref[...], sup_ref[...], sdn_ref[...])
        else:
            v0 = c0 - q_cols - kv_cols
            kv_ref[1, :, v0:v0 + cw] = acc


def _qkv_rope_call(x2d, shift, scale, w, rope_tabs, *, q_cols, kv_cols, q_scale, tm,
                   tiles_per_group, name):
    t, d = x2d.shape
    n = q_cols + 2 * kv_cols
    assert w.shape == (d, n) and q_cols % MXU_COLS == 0 and kv_cols % MXU_COLS == 0
    r = shift.shape[1]
    n_tab = rope_tabs[0].shape[0] // tm
    mod_spec = pl.BlockSpec((1, r, d), lambda i: (i // tiles_per_group, 0, 0))
    tab_spec = pl.BlockSpec((tm, LANES), lambda i: (i % n_tab, 0))
    vmem = (2 * tm * d * 4 + tm * d * 2 + 2 * d * n * 2 + 2 * tm * q_cols * 2 + 4 * tm * kv_cols * 4
            + 8 * tm * MXU_COLS * 4 + 6 * tm * LANES * 4 + 4 * MIB)
    return pl.pallas_call(
        functools.partial(_qkv_rope_kernel, q_scale=q_scale),
        grid=(t // tm,),
        in_specs=[pl.BlockSpec((tm, d), lambda i: (i, 0)), mod_spec, mod_spec,
                  pl.BlockSpec((d, n), lambda i: (0, 0)), tab_spec, tab_spec, tab_spec],
        out_specs=[pl.BlockSpec((tm, q_cols), lambda i: (i, 0)),
                   pl.BlockSpec((2, tm, kv_cols), lambda i: (0, i, 0))],
        out_shape=[jax.ShapeDtypeStruct((t, q_cols), BF16),
                   jax.ShapeDtypeStruct((2, t, kv_cols), F32)],
        scratch_shapes=[pltpu.VMEM((tm, d), BF16)],
        compiler_params=_params(("arbitrary",), vmem),
        name=name,
    )(x2d, shift, scale, w, *rope_tabs)


def _block_starts(qi, *, tq, q_offset, n_prev, nvar):
    if isinstance(qi, int):
        qstart = qi * tq
        return qstart, max(qstart + q_offset - n_prev * CHUNK, 0), min(qi, nvar - 1)
    qstart = pl.multiple_of(qi * tq, tq)
    kstart = pl.multiple_of(jnp.maximum(qstart + (q_offset - n_prev * CHUNK), 0), CHUNK)
    return qstart, kstart, jnp.minimum(qi, nvar - 1)


def _pipelined_blocks(nqb, scores, finish):
    scores(0, 0)
    if nqb == 1:
        finish(0, 0)
        return
    assert nqb % 2 == 0

    def body(t, carry):
        qi = 2 * t
        scores(qi + 1, 1)
        finish(qi, 0)
        scores(jnp.minimum(qi + 2, nqb - 1), 0)
        finish(qi + 1, 1)
        return carry
    lax.fori_loop(0, nqb // 2, body, 0)


def _band_valid(*, tq, w, qoff, n_prev, n_valid):
    i = np.arange(tq)[:, None]
    j = np.arange(w)[None, :]
    cq = (qoff + i) // CHUNK
    ck = j // CHUNK
    return (ck <= cq) & (ck >= cq - n_prev) & (j < n_valid)


def _num_variants(*, tq, q_offset, n_prev):
    return -(-(n_prev * CHUNK - q_offset) // tq) + 1


def _attn_a_kernel(q_ref, k_ref, v_ref, g_ref, o_ref, kb_scr, vb_scr, s0_scr, s1_scr, *, tq, w,
                   nqb, q_offset, nvar):
    sk = kb_scr.shape[0]
    for r0 in range(0, sk, CAST_ROWS):
        r1 = min(r0 + CAST_ROWS, sk)
        kb_scr[r0:r1, :] = k_ref[0, 0, r0:r1, :].astype(BF16)
        vb_scr[r0:r1, :] = v_ref[0, 0, r0:r1, :].astype(BF16)
    s_scr = (s0_scr, s1_scr)
    starts = functools.partial(_block_starts, tq=tq, q_offset=q_offset, n_prev=N_PREV_A, nvar=nvar)

    def scores(qi, slot):
        qstart, kstart, var = starts(qi)
        q = q_ref[0, pl.ds(qstart, tq), :]
        kw = kb_scr[pl.ds(kstart, w), :]
        s = lax.dot_general(q, kw, (((1,), (1,)), ((), ())), preferred_element_type=F32)
        s_scr[slot][...] = s + g_ref[0, var]

    def finish(qi, slot):
        qstart, kstart, _ = starts(qi)
        s = s_scr[slot][...]
        m = jnp.max(s, axis=-1, keepdims=True)
        e = jnp.exp2(s - m)
        l = jnp.sum(e, axis=-1, keepdims=True)
        o = jnp.dot(e.astype(BF16), vb_scr[pl.ds(kstart, w), :], preferred_element_type=F32)
        o_ref[0, pl.ds(qstart, tq), :] = (o * (1.0 / l)).astype(BF16)

    _pipelined_blocks(nqb, scores, finish)


def _bias_tiles_a(rel_table, *, tq, w, q_offset, nvar, n_valid):
    band = N_PREV_A * CHUNK
    n_heads = rel_table.shape[0]
    span = tq + w - 1
    tiles = []
    for v in range(nvar):
        qoff = min(q_offset + v * tq, band)
        dist = qoff - ((np.arange(span) + tq - 1) % span - (tq - 1))
        rp = rel_table[:, np.clip(dist, -REL_CLIP, REL_CLIP) + REL_CLIP] * LOG2E
        toep = jnp.tile(rp, (1, tq))[:, :tq * (span - 1)].reshape(n_heads, tq, span - 1)[:, :, :w]
        valid = _band_valid(tq=tq, w=w, qoff=qoff, n_prev=N_PREV_A, n_valid=n_valid)
        tiles.append(jnp.where(valid[None], toep, NEG_INF))
    return jnp.stack(tiles, axis=1).astype(F32)


def _attn_a_call(q, kv, rel_table, *, tq, q_offset, n_valid, name):
    b, sq, _ = q.shape
    sk = kv.shape[2]
    band = N_PREV_A * CHUNK
    w = band + tq
    nqb = sq // tq
    nvar = _num_variants(tq=tq, q_offset=q_offset, n_prev=N_PREV_A)
    assert nqb == 1 or n_valid == sk
    assert (nqb - 1) * tq + q_offset - band + w <= sk
    g = _bias_tiles_a(rel_table, tq=tq, w=w, q_offset=q_offset, nvar=nvar, n_valid=n_valid)
    vmem = (2 * (2 * sq * DH_A * 2 + 2 * sk * DH_A * 4 + nvar * tq * w * 4) + 2 * sk * DH_A * 2
            + 12 * tq * w * 4 + 4 * MIB)
    return pl.pallas_call(
        functools.partial(_attn_a_kernel, tq=tq, w=w, nqb=nqb, q_offset=q_offset, nvar=nvar),
        grid=(H_A, b),
        in_specs=[
            pl.BlockSpec((1, sq, DH_A), lambda h, bi: (bi, 0, h)),
            pl.BlockSpec((1, 1, sk, DH_A), lambda h, bi: (0, bi, 0, h)),
            pl.BlockSpec((1, 1, sk, DH_A), lambda h, bi: (1, bi, 0, h)),
            pl.BlockSpec((1, nvar, tq, w), lambda h, bi: (h, 0, 0, 0)),
        ],
        out_specs=pl.BlockSpec((1, sq, DH_A), lambda h, bi: (bi, 0, h)),
        out_shape=jax.ShapeDtypeStruct(q.shape, BF16),
        scratch_shapes=[pltpu.VMEM((sk, DH_A), BF16), pltpu.VMEM((sk, DH_A), BF16),
                        pltpu.VMEM((tq, w), F32), pltpu.VMEM((tq, w), F32)],
        compiler_params=_params(("arbitrary", "arbitrary"), vmem),
        name=name,
    )(q, kv, kv, g)


def _attn_b_kernel(sink_ref, q_ref, k_ref, v_ref, mask_ref, o_ref, kd_scr, vd_scr, s0_scr, s1_scr,
                   *, tq, w, nqb, q_offset, nvar):
    j = pl.program_id(0)
    sk = kd_scr.shape[0]
    rows = min(CAST_ROWS, sk)
    assert sk % rows == 0
    n_blk = G_B // 2

    own_half = (lax.broadcasted_iota(jnp.int32, (rows, LANES), 1) >= DH_B).astype(jnp.int32) == (j & 1)
    for r0 in range(0, sk, rows):
        for src, dst in ((k_ref, kd_scr), (v_ref, vd_scr)):
            x = src[0, 0, r0:r0 + rows, :]
            dst[r0:r0 + rows, :] = jnp.where(own_half, x, pltpu.roll(x, DH_B, 1)).astype(BF16)

    lane_lo = lax.broadcasted_iota(jnp.int32, (tq, LANES), 1) < DH_B
    row = lax.broadcasted_iota(jnp.int32, (G_B * tq, 1), 0)
    skcol = jnp.full((G_B * tq, 1), sink_ref[G_B * j + G_B - 1], F32)
    for g in range(G_B - 2, -1, -1):
        skcol = jnp.where(row < (g + 1) * tq, sink_ref[G_B * j + g], skcol)
    skcol = skcol * LOG2E

    s_scr = (s0_scr, s1_scr)
    starts = functools.partial(_block_starts, tq=tq, q_offset=q_offset, n_prev=N_PREV_B, nvar=nvar)

    def scores(qi, slot):
        qstart, kstart, var = starts(qi)
        kw = kd_scr[pl.ds(kstart, w), :]
        parts = []
        for c in range(n_blk):
            qc = q_ref[0, pl.ds(qstart, tq), c * LANES:(c + 1) * LANES]
            zero = jnp.zeros_like(qc)
            parts += [jnp.where(lane_lo, qc, zero), jnp.where(lane_lo, zero, qc)]
        lhs = jnp.concatenate(parts, axis=0)
        s = lax.dot_general(lhs, kw, (((1,), (1,)), ((), ())), preferred_element_type=F32)
        s_scr[slot][...] = (s.reshape(G_B, tq, w) + mask_ref[var][None]).reshape(G_B * tq, w)

    def finish(qi, slot):
        qstart, kstart, _ = starts(qi)
        s = s_scr[slot][...]
        m = jnp.maximum(jnp.max(s, axis=-1, keepdims=True), skcol)
        e = jnp.exp2(s - m)
        den = jnp.sum(e, axis=-1, keepdims=True) + jnp.exp2(skcol - m)
        o2 = jnp.dot(e.astype(BF16), vd_scr[pl.ds(kstart, w), :], preferred_element_type=F32)
        o2 = o2 * (1.0 / den)
        for c in range(n_blk):
            lo = o2[2 * c * tq:(2 * c + 1) * tq]
            hi = o2[(2 * c + 1) * tq:(2 * c + 2) * tq]
            o_ref[0, pl.ds(qstart, tq), c * LANES:(c + 1) * LANES] = (
                jnp.where(lane_lo, lo, hi).astype(BF16))

    _pipelined_blocks(nqb, scores, finish)


def _attn_b_call(q, kv, sink, *, tq, q_offset, n_valid, name):
    b, sq, _ = q.shape
    sk = kv.shape[2]
    band = N_PREV_B * CHUNK
    w = band + tq
    nqb = sq // tq
    nvar = _num_variants(tq=tq, q_offset=q_offset, n_prev=N_PREV_B)
    qw = G_B * DH_B
    assert nqb == 1 or n_valid == sk
    assert (nqb - 1) * tq + q_offset - band + w <= sk
    masks = np.stack([
        np.where(_band_valid(tq=tq, w=w, qoff=min(q_offset + v * tq, band), n_prev=N_PREV_B,
                             n_valid=n_valid), 0.0, NEG_INF) for v in range(nvar)]).astype(np.float32)
    vmem = (2 * (2 * sq * qw * 2 + 2 * sk * LANES * 4 + nvar * tq * w * 4) + 2 * sk * LANES * 2
            + 12 * G_B * tq * w * 4 + 4 * MIB)
    return pl.pallas_call(
        functools.partial(_attn_b_kernel, tq=tq, w=w, nqb=nqb, q_offset=q_offset, nvar=nvar),
        grid=(KV_B, b),
        in_specs=[
            pl.BlockSpec(memory_space=pltpu.SMEM),
            pl.BlockSpec((1, sq, qw), lambda j, bi: (bi, 0, j)),
            pl.BlockSpec((1, 1, sk, LANES), lambda j, bi: (0, bi, 0, j // 2)),
            pl.BlockSpec((1, 1, sk, LANES), lambda j, bi: (1, bi, 0, j // 2)),
            pl.BlockSpec((nvar, tq, w), lambda j, bi: (0, 0, 0)),
        ],
        out_specs=pl.BlockSpec((1, sq, qw), lambda j, bi: (bi, 0, j)),
        out_shape=jax.ShapeDtypeStruct(q.shape, BF16),
        scratch_shapes=[pltpu.VMEM((sk, LANES), BF16), pltpu.VMEM((sk, LANES), BF16),
                        pltpu.VMEM((G_B * tq, w), F32), pltpu.VMEM((G_B * tq, w), F32)],
        compiler_params=_params(("arbitrary", "arbitrary"), vmem),
        name=name,
    )(sink, q, kv, kv, jnp.asarray(masks))


def _wo_ln_kernel(o_ref, x_ref, gate_ref, w_ref, g_ref, b_ref, out_ref):
    y = jnp.dot(o_ref[...], w_ref[...], preferred_element_type=F32)
    z = ALPHA * x_ref[...] + gate_ref[0] * y
    out_ref[...] = _layer_norm(z, g_ref[...], b_ref[...])


def _wo_ln_call(o2d, x2d, gate, w, ln_g, ln_b, *, tm, tiles_per_group, name):
    t, d = x2d.shape
    r = gate.shape[1]
    vmem = 2 * (tm * d * 2 + 2 * tm * d * 4 + d * d * 2) + 4 * tm * d * 4 + 4 * MIB
    return pl.pallas_call(
        _wo_ln_kernel,
        grid=(t // tm,),
        in_specs=[
            pl.BlockSpec((tm, d), lambda i: (i, 0)),
            pl.BlockSpec((tm, d), lambda i: (i, 0)),
            pl.BlockSpec((1, r, d), lambda i: (i // tiles_per_group, 0, 0)),
            pl.BlockSpec((d, d), lambda i: (0, 0)),
            pl.BlockSpec((1, d), lambda i: (0, 0)),
            pl.BlockSpec((1, d), lambda i: (0, 0)),
        ],
        out_specs=pl.BlockSpec((tm, d), lambda i: (i, 0)),
        out_shape=jax.ShapeDtypeStruct((t, d), F32),
        compiler_params=_params(("arbitrary",), vmem),
        name=name,
    )(o2d, x2d, gate, w, ln_g, ln_b)


def _ffn_kernel(x_ref, shift_ref, scale_ref, gate_ref, wu_ref, wd_ref, g_ref, b_ref,
                out_ref, h_scr):
    j = pl.program_id(1)

    @pl.when(j == 0)
    def _():
        h = x_ref[...] * (1.0 + scale_ref[0]) + shift_ref[0]
        h_scr[...] = h.astype(BF16)
        out_ref[...] = jnp.zeros_like(out_ref)

    u = jnp.maximum(jnp.dot(h_scr[...], wu_ref[...], preferred_element_type=F32), 0.0)
    ub = (u * u).astype(BF16)
    d = out_ref.shape[1]
    for c in range(d // FFN_DOWN_COLS):
        cols = slice(c * FFN_DOWN_COLS, (c + 1) * FFN_DOWN_COLS)
        out_ref[:, cols] += jnp.dot(ub, wd_ref[:, cols], preferred_element_type=F32)

    @pl.when(j == pl.num_programs(1) - 1)
    def _():
        z = ALPHA * x_ref[...] + gate_ref[0] * out_ref[...]
        out_ref[...] = _layer_norm(z, g_ref[...], b_ref[...])


def _ffn_call(x2d, shift, scale, gate, w_up, w_down, ln_g, ln_b, *, tm, tf, tiles_per_group, name):
    t, d = x2d.shape
    dff = w_up.shape[1]
    r = gate.shape[1]
    mod_spec = pl.BlockSpec((1, r, d), lambda i, j: (i // tiles_per_group, 0, 0))
    vmem = 4 * tm * d * 4 + tm * d * 2 + 4 * d * tf * 2 + 3 * tm * tf * 4 + 2 * tm * d * 4 + 4 * MIB
    return pl.pallas_call(
        _ffn_kernel,
        grid=(t // tm, dff // tf),
        in_specs=[
            pl.BlockSpec((tm, d), lambda i, j: (i, 0)),
            mod_spec, mod_spec, mod_spec,
            pl.BlockSpec((d, tf), lambda i, j: (0, j)),
            pl.BlockSpec((tf, d), lambda i, j: (j, 0)),
            pl.BlockSpec((1, d), lambda i, j: (0, 0)),
            pl.BlockSpec((1, d), lambda i, j: (0, 0)),
        ],
        out_specs=pl.BlockSpec((tm, d), lambda i, j: (i, 0)),
        out_shape=jax.ShapeDtypeStruct((t, d), F32),
        scratch_shapes=[pltpu.VMEM((tm, d), BF16)],
        compiler_params=_params(("arbitrary", "arbitrary"), vmem),
        name=name,
    )(x2d, shift, scale, gate, w_up, w_down, ln_g, ln_b)


def _rope_tables(pos):
    half = ROT_DIM // 2
    freqs = ROPE_THETA ** (-jnp.arange(0, ROT_DIM, 2, dtype=F32) / ROT_DIM)
    ang = pos.astype(F32)[:, None] * freqs[None, :]
    cos, sin = jnp.cos(ang), jnp.sin(ang)
    n = pos.shape[0]
    ones = jnp.ones((n, DH_B - ROT_DIM), F32)
    zeros = jnp.zeros((n, DH_B - ROT_DIM), F32)
    z8 = jnp.zeros((n, half), F32)
    cos_h = jnp.concatenate([cos, cos, ones], axis=1)
    sup_h = jnp.concatenate([-sin, z8, zeros], axis=1)
    sdn_h = jnp.concatenate([z8, sin, zeros], axis=1)
    return tuple(jnp.concatenate([t, t], axis=1) for t in (cos_h, sup_h, sdn_h))


def _trunk(x, mod, pos0, hist, wts, *, tm, tf, tq_a, tq_b):
    bn, s, d = x.shape
    t = bn * s
    per_row = tm > s
    tm_wo = min(tm, 512)
    if per_row:
        assert t == tm
        tiles_per_group = 1

        def mod3(m):
            return jnp.repeat(m, s, axis=0).reshape(1, t, d)
    else:
        tiles_per_group = s // tm

        def mod3(m):
            return m.reshape(bn, 1, d)

    def split_mod(p):
        m = mod[p]
        return mod3(m[:, :d]), mod3(m[:, d:2 * d]), mod3(m[:, 2 * d:])

    x2d = x.reshape(t, d)
    pos = pos0 + jnp.arange(s)
    rope_tabs = _rope_tables(pos)
    if per_row:
        rope_tabs = tuple(jnp.tile(tb, (bn, 1)) for tb in rope_tabs)

    def pad_rows(a, n):
        return jnp.pad(a, ((0, 0), (0, n - a.shape[1]), (0, 0)))

    shift, scale, gate = split_mod(0)
    q, kv = _qkv_call(x2d, shift, scale, wts["w_qkv_a"], q_cols=d, kv_cols=d,
                      q_scale=DH_A ** -0.5 * LOG2E, tm=tm, tn=min(1024, d),
                      tiles_per_group=tiles_per_group, name="qkv_a")
    new_a = kv.reshape(2, bn, s, H_A, DH_A)[:, None, :, -N_PREV_A * CHUNK:]
    q = q.reshape(bn, s, d)
    kv = kv.reshape(2, bn, s, d)
    if hist is None:
        o = _attn_a_call(q, kv, wts["rel_bias_a"], tq=tq_a, q_offset=0, n_valid=s, name="attn_a")
    else:
        hk = hist["a_k"].reshape(bn, -1, d)
        hv = hist["a_v"].reshape(bn, -1, d)
        hl = hk.shape[1]
        sk = N_PREV_A * CHUNK + tq_a
        kv_ext = jnp.stack([pad_rows(jnp.concatenate([hk, kv[0]], axis=1), sk),
                            pad_rows(jnp.concatenate([hv, kv[1]], axis=1), sk)])
        o = _attn_a_call(pad_rows(q, tq_a), kv_ext, wts["rel_bias_a"], tq=tq_a,
                         q_offset=hl, n_valid=hl + s, name="attn_a")[:, :s]
    x2d = _wo_ln_call(o.reshape(t, d), x2d, gate, wts["w_o_a"], wts["ln_g"][0, 0], wts["ln_b"][0, 0],
                      tm=tm_wo, tiles_per_group=tiles_per_group * (tm // tm_wo), name="wo_ln_a")
    shift, scale, gate = split_mod(1)
    x2d = _ffn_call(x2d, shift, scale, gate, wts["w_up"][0], wts["w_down"][0],
                    wts["ln_g"][0, 1], wts["ln_b"][0, 1], tm=tm, tf=tf,
                    tiles_per_group=tiles_per_group, name="ffn_0")

    shift, scale, gate = split_mod(2)
    kvw = KV_B * DH_B
    q, kv = _qkv_rope_call(x2d, shift, scale, wts["w_qkv_b"], rope_tabs, q_cols=d, kv_cols=kvw,
                           q_scale=DH_B ** -0.5 * LOG2E, tm=tm_wo,
                           tiles_per_group=tiles_per_group * (tm // tm_wo), name="qkv_b")
    new_b = kv.reshape(2, bn, s, KV_B, DH_B)[:, None, :, -N_PREV_B * CHUNK:]
    q = q.reshape(bn, s, d)
    kv = kv.reshape(2, bn, s, kvw)
    if hist is None:
        o = _attn_b_call(q, kv, wts["sink_b"], tq=tq_b, q_offset=0, n_valid=s, name="attn_b")
    else:
        hk = hist["b_k"].reshape(bn, -1, kvw)
        hv = hist["b_v"].reshape(bn, -1, kvw)
        hl = hk.shape[1]
        sk = N_PREV_B * CHUNK + tq_b
        kv_ext = jnp.stack([pad_rows(jnp.concatenate([hk, kv[0]], axis=1), sk),
                            pad_rows(jnp.concatenate([hv, kv[1]], axis=1), sk)])
        o = _attn_b_call(pad_rows(q, tq_b), kv_ext, wts["sink_b"], tq=tq_b,
                         q_offset=hl, n_valid=hl + s, name="attn_b")[:, :s]
    x2d = _wo_ln_call(o.reshape(t, d), x2d, gate, wts["w_o_b"], wts["ln_g"][1, 0], wts["ln_b"][1, 0],
                      tm=tm_wo, tiles_per_group=tiles_per_group * (tm // tm_wo), name="wo_ln_b")
    shift, scale, gate = split_mod(3)
    x2d = _ffn_call(x2d, shift, scale, gate, wts["w_up"][1], wts["w_down"][1],
                    wts["ln_g"][1, 1], wts["ln_b"][1, 1], tm=tm, tf=tf,
                    tiles_per_group=tiles_per_group, name="ffn_1")
    return x2d.reshape(bn, s, d), new_a[0], new_a[1], new_b[0], new_b[1]


def kernel(x_prompt, x_sample, cache_a_k, cache_a_v, cache_b_k, cache_b_v, c_prompt, c_sample,
           w_ada, b_ada, ln_g, ln_b, w_qkv_a, w_o_a, rel_bias_a, w_qkv_b, w_o_b, sink_b, w_up, w_down):
    d = D_MODEL
    bp = c_prompt.shape[0]
    c_all = jnp.concatenate([c_prompt, c_sample], axis=0)
    mod = _ada_call(c_all, w_ada.reshape(DEPTH * 2, d, 3 * d), b_ada.reshape(DEPTH * 2, 1, 3 * d))

    wts = {
        "w_qkv_a": w_qkv_a[0].astype(BF16),
        "w_o_a": w_o_a[0].astype(BF16),
        "w_qkv_b": w_qkv_b[0].astype(BF16),
        "w_o_b": w_o_b[0].astype(BF16),
        "w_up": [w_up[i].astype(BF16) for i in range(DEPTH)],
        "w_down": [w_down[i].astype(BF16) for i in range(DEPTH)],
        "rel_bias_a": rel_bias_a[0],
        "sink_b": sink_b[0],
        "ln_g": ln_g.reshape(DEPTH, 2, 1, d),
        "ln_b": ln_b.reshape(DEPTH, 2, 1, d),
    }

    y_p, ak_p, av_p, bk_p, bv_p = _trunk(
        x_prompt, mod[:, :bp], 0, None, wts, tm=1024, tf=512, tq_a=256, tq_b=128)
    hist = {"a_k": cache_a_k[0], "a_v": cache_a_v[0], "b_k": cache_b_k[0], "b_v": cache_b_v[0]}
    n_s = x_sample.shape[0] * x_sample.shape[1]
    y_s, ak_s, av_s, bk_s, bv_s = _trunk(
        x_sample, mod[:, bp:], PAST_LEN, hist, wts, tm=n_s, tf=512, tq_a=CHUNK, tq_b=2 * CHUNK)
    return (y_p, y_s, ak_p, av_p, bk_p, bv_p, ak_s, av_s, bk_s, bv_s)
```

```python
import functools
import math

import numpy as np
import jax
import jax.numpy as jnp
from jax import lax
from jax.experimental import pallas as pl
from jax.experimental.pallas import tpu as pltpu

F32 = jnp.float32
BF16 = jnp.bfloat16

D_MODEL = 2048
DEPTH = 2
PAST_LEN = 1024
CHUNK = 64
H_A = 16
DH_A = D_MODEL // H_A
N_PREV_A = 8
REL_CLIP = 128
DH_B = 64
H_B = D_MODEL // DH_B
KV_B = H_B // 4
G_B = H_B // KV_B
N_PREV_B = 2
ROT_DIM = DH_B // 4
ROPE_THETA = 500000.0
D_FF = 4 * D_MODEL
ALPHA = (2 * DEPTH) ** 0.25
LN_EPS = 1e-5
NEG_INF = -1e30
LOG2E = math.log2(math.e)

LANES = 128
MXU_COLS = 256
MIB = 1024 * 1024
VMEM_CAP_BYTES = 56 * MIB
FFN_DOWN_COLS = 512
CAST_ROWS = 512
EPILOGUE_ROWS = 128
FFN_LAST_ROWS = 256


def _params(semantics, vmem_bytes):
    return pltpu.CompilerParams(
        dimension_semantics=semantics,
        vmem_limit_bytes=int(min(VMEM_CAP_BYTES, vmem_bytes)))


def _layer_norm(z, g, b):
    mu = jnp.mean(z, axis=-1, keepdims=True)
    zc = z - mu
    var = jnp.mean(zc * zc, axis=-1, keepdims=True)
    return zc * lax.rsqrt(var + LN_EPS) * g + b


def _ada_kernel(c_ref, w_ref, b_ref, o_ref):
    c = c_ref[...]
    sc = (c * (1.0 / (1.0 + jnp.exp(-c)))).astype(BF16)
    o_ref[0] = jnp.dot(sc, w_ref[0].astype(BF16), preferred_element_type=F32) + b_ref[0]


def _ada_call(c_all, w4, b4, *, tn=1024):
    n_pairs, d, n = w4.shape
    rows = c_all.shape[0]
    return pl.pallas_call(
        _ada_kernel,
        grid=(n_pairs, n // tn),
        in_specs=[
            pl.BlockSpec((rows, d), lambda p, j: (0, 0)),
            pl.BlockSpec((1, d, tn), lambda p, j: (p, 0, j)),
            pl.BlockSpec((1, 1, tn), lambda p, j: (p, 0, j)),
        ],
        out_specs=pl.BlockSpec((1, rows, tn), lambda p, j: (p, 0, j)),
        out_shape=jax.ShapeDtypeStruct((n_pairs, rows, n), F32),
        compiler_params=_params(("arbitrary", "arbitrary"), 2 * d * tn * 4 + 16 * MIB),
        name="ada_mod",
    )(c_all, w4, b4)


def _rope(acc, cos_t, sin_up, sin_dn):
    half = ROT_DIM // 2
    parts = []
    for c in range(acc.shape[1] // LANES):
        a = acc[:, c * LANES:(c + 1) * LANES]
        parts.append(a * cos_t
                     + pltpu.roll(a, LANES - half, 1) * sin_up
                     + pltpu.roll(a, half, 1) * sin_dn)
    return jnp.concatenate(parts, axis=1)


def _qkv_kernel(x_ref, shift_ref, scale_ref, w_ref, q_ref, kv_ref, h_scr, *, nq, q_scale):
    j = pl.program_id(1)
    tn = w_ref.shape[1]
    cw = min(MXU_COLS, tn)

    @pl.when(j == 0)
    def _():
        h = x_ref[...] * (1.0 + scale_ref[0]) + shift_ref[0]
        h_scr[...] = h.astype(BF16)

    def for_chunks(store):
        for c in range(tn // cw):
            cols = slice(c * cw, (c + 1) * cw)
            store(cols, jnp.dot(h_scr[...], w_ref[:, cols], preferred_element_type=F32))

    @pl.when(j < nq)
    def _():
        def store(cols, acc):
            q_ref[:, cols] = (acc * q_scale).astype(BF16)
        for_chunks(store)

    @pl.when(j >= nq)
    def _():
        def store(cols, acc):
            kv_ref[0, :, cols] = acc
        for_chunks(store)


def _qkv_call(x2d, shift, scale, w, *, q_cols, kv_cols, q_scale, tm, tn, tiles_per_group, name):
    t, d = x2d.shape
    nq, nkv = q_cols // tn, kv_cols // tn
    r = shift.shape[1]
    mod_spec = pl.BlockSpec((1, r, d), lambda i, j: (i // tiles_per_group, 0, 0))

    def kv_map(i, j):
        jj = jnp.maximum(j - nq, 0)
        return (jj // nkv, i, jj % nkv)

    vmem = 2 * tm * d * 4 + tm * d * 2 + 2 * d * tn * 2 + 2 * tm * tn * 6 + 8 * tm * MXU_COLS * 4 + 8 * MIB
    return pl.pallas_call(
        functools.partial(_qkv_kernel, nq=nq, q_scale=q_scale),
        grid=(t // tm, nq + 2 * nkv),
        in_specs=[pl.BlockSpec((tm, d), lambda i, j: (i, 0)), mod_spec, mod_spec,
                  pl.BlockSpec((d, tn), lambda i, j: (0, j))],
        out_specs=[pl.BlockSpec((tm, tn), lambda i, j: (i, jnp.minimum(j, nq - 1))),
                   pl.BlockSpec((1, tm, tn), kv_map)],
        out_shape=[jax.ShapeDtypeStruct((t, q_cols), BF16),
                   jax.ShapeDtypeStruct((2, t, kv_cols), F32)],
        scratch_shapes=[pltpu.VMEM((tm, d), BF16)],
        compiler_params=_params(("arbitrary", "arbitrary"), vmem),
        name=name,
    )(x2d, shift, scale, w)


def _qkv_rope_kernel(x_ref, shift_ref, scale_ref, w_ref, cos_ref, sup_ref, sdn_ref, q_ref, kv_ref,
                     h_scr, *, q_scale):
    h = x_ref[...] * (1.0 + scale_ref[0]) + shift_ref[0]
    h_scr[...] = h.astype(BF16)
    q_cols, kv_cols = q_ref.shape[1], kv_ref.shape[2]
    cw = MXU_COLS
    for c0 in range(0, q_cols + 2 * kv_cols, cw):
        acc = jnp.dot(h_scr[...], w_ref[:, c0:c0 + cw], preferred_element_type=F32)
        if c0 < q_cols:
            rot = _rope(acc, cos_ref[...], sup_ref[...], sdn_ref[...])
            q_ref[:, c0:c0 + cw] = (rot * q_scale).astype(BF16)
        elif c0 < q_cols + kv_cols:
            k0 = c0 - q_cols
            kv_ref[0, :, k0:k0 + cw] = _rope(acc, cos_ref[...], sup_ref[...], sdn_ref[...])
        else:
            v0 = c0 - q_cols - kv_cols
            kv_ref[1, :, v0:v0 + cw] = acc


def _qkv_rope_call(x2d, shift, scale, w, rope_tabs, *, q_cols, kv_cols, q_scale, tm,
                   tiles_per_group, name):
    t, d = x2d.shape
    n = q_cols + 2 * kv_cols
    assert w.shape == (d, n) and q_cols % MXU_COLS == 0 and kv_cols % MXU_COLS == 0
    r = shift.shape[1]
    n_tab = rope_tabs[0].shape[0] // tm
    mod_spec = pl.BlockSpec((1, r, d), lambda i: (i // tiles_per_group, 0, 0))
    tab_spec = pl.BlockSpec((tm, LANES), lambda i: (i % n_tab, 0))
    vmem = (2 * tm * d * 4 + tm * d * 2 + 2 * d * n * 2 + 2 * tm * q_cols * 2 + 4 * tm * kv_cols * 4
            + 8 * tm * MXU_COLS * 4 + 6 * tm * LANES * 4 + 4 * MIB)
    return pl.pallas_call(
        functools.partial(_qkv_rope_kernel, q_scale=q_scale),
        grid=(t // tm,),
        in_specs=[pl.BlockSpec((tm, d), lambda i: (i, 0)), mod_spec, mod_spec,
                  pl.BlockSpec((d, n), lambda i: (0, 0)), tab_spec, tab_spec, tab_spec],
        out_specs=[pl.BlockSpec((tm, q_cols), lambda i: (i, 0)),
                   pl.BlockSpec((2, tm, kv_cols), lambda i: (0, i, 0))],
        out_shape=[jax.ShapeDtypeStruct((t, q_cols), BF16),
                   jax.ShapeDtypeStruct((2, t, kv_cols), F32)],
        scratch_shapes=[pltpu.VMEM((tm, d), BF16)],
        compiler_params=_params(("arbitrary",), vmem),
        name=name,
    )(x2d, shift, scale, w, *rope_tabs)


def _block_starts(qi, *, tq, q_offset, n_prev, nvar):
    if isinstance(qi, int):
        qstart = qi * tq
        return qstart, max(qstart + q_offset - n_prev * CHUNK, 0), min(qi, nvar - 1)
    qstart = pl.multiple_of(qi * tq, tq)
    kstart = pl.multiple_of(jnp.maximum(qstart + (q_offset - n_prev * CHUNK), 0), CHUNK)
    return qstart, kstart, jnp.minimum(qi, nvar - 1)


def _pipelined_blocks(nqb, scores, finish, n_static=0):
    scores(0, 0)
    if nqb == 1:
        finish(0, 0)
        return
    assert nqb % 2 == 0 and 0 <= n_static <= nqb // 2

    def two_blocks(qi, nxt):
        scores(qi + 1, 1)
        finish(qi, 0)
        scores(nxt, 0)
        finish(qi + 1, 1)

    for t in range(n_static):
        two_blocks(2 * t, min(2 * t + 2, nqb - 1))

    def body(t, carry):
        two_blocks(2 * t, jnp.minimum(2 * t + 2, nqb - 1))
        return carry
    lax.fori_loop(n_static, nqb // 2, body, 0)


def _band_valid(*, tq, w, qoff, n_prev, n_valid):
    i = np.arange(tq)[:, None]
    j = np.arange(w)[None, :]
    cq = (qoff + i) // CHUNK
    ck = j // CHUNK
    return (ck <= cq) & (ck >= cq - n_prev) & (j < n_valid)


def _num_variants(*, tq, q_offset, n_prev):
    return -(-(n_prev * CHUNK - q_offset) // tq) + 1


def _attn_a_kernel(q_ref, k_ref, v_ref, g_ref, o_ref, kb_scr, vb_scr, s0_scr, s1_scr, *, tq, w,
                   nqb, q_offset, nvar):
    sk = kb_scr.shape[0]
    for r0 in range(0, sk, CAST_ROWS):
        r1 = min(r0 + CAST_ROWS, sk)
        kb_scr[r0:r1, :] = k_ref[0, 0, r0:r1, :].astype(BF16)
        vb_scr[r0:r1, :] = v_ref[0, 0, r0:r1, :].astype(BF16)
    s_scr = (s0_scr, s1_scr)
    band = N_PREV_A * CHUNK
    starts = functools.partial(_block_starts, tq=tq, q_offset=q_offset, n_prev=N_PREV_A, nvar=nvar)

    def scores(qi, slot):
        qstart, kstart, var = starts(qi)
        if not isinstance(var, int):
            var = nvar - 1
        off = band - min(q_offset + var * tq, band)
        q = q_ref[0, pl.ds(qstart, tq), :]
        kw = kb_scr[pl.ds(kstart, w), :]
        s = lax.dot_general(q, kw, (((1,), (1,)), ((), ())), preferred_element_type=F32)
        s_scr[slot][...] = s + g_ref[0, :, off:off + w]

    def finish(qi, slot):
        qstart, kstart, _ = starts(qi)
        s = s_scr[slot][...]
        m = jnp.max(s, axis=-1, keepdims=True)
        e = jnp.exp2(s - m)
        l = jnp.sum(e, axis=-1, keepdims=True)
        o = jnp.dot(e.astype(BF16), vb_scr[pl.ds(kstart, w), :], preferred_element_type=F32)
        o_ref[0, pl.ds(qstart, tq), :] = (o * (1.0 / l)).astype(BF16)

    _pipelined_blocks(nqb, scores, finish, n_static=nvar // 2)


def _bias_wide_a(rel_table, *, tq, w, q_offset):
    band = N_PREV_A * CHUNK
    n_heads = rel_table.shape[0]
    wide = w + band - min(q_offset, band)
    span = tq + wide - 1
    dist = band - ((np.arange(span) + tq - 1) % span - (tq - 1))
    rp = rel_table[:, np.clip(dist, -REL_CLIP, REL_CLIP) + REL_CLIP] * LOG2E
    toep = jnp.tile(rp, (1, tq))[:, :tq * (span - 1)].reshape(n_heads, tq, span - 1)[:, :, :wide]
    dchunk = (np.arange(wide)[None, :] - band) // CHUNK - np.arange(tq)[:, None] // CHUNK
    valid = (dchunk <= 0) & (dchunk >= -N_PREV_A)
    return jnp.where(valid[None], toep, NEG_INF).astype(F32)


def _attn_a_call(q, kv, rel_table, *, tq, name):
    b, sq, _ = q.shape
    sk = kv.shape[2]
    band = N_PREV_A * CHUNK
    w = band + tq
    nqb = sq // tq
    q_offset = 0
    nvar = _num_variants(tq=tq, q_offset=q_offset, n_prev=N_PREV_A)
    assert sq == sk and sq % tq == 0 and tq % LANES == 0 and w <= sk
    g = _bias_wide_a(rel_table, tq=tq, w=w, q_offset=q_offset)
    wide = g.shape[2]
    vmem = (2 * (2 * sq * DH_A * 2 + 2 * sk * DH_A * 4 + tq * wide * 4) + 2 * sk * DH_A * 2
            + 12 * tq * w * 4 + 4 * MIB)
    return pl.pallas_call(
        functools.partial(_attn_a_kernel, tq=tq, w=w, nqb=nqb, q_offset=q_offset, nvar=nvar),
        grid=(H_A, b),
        in_specs=[
            pl.BlockSpec((1, sq, DH_A), lambda h, bi: (bi, 0, h)),
            pl.BlockSpec((1, 1, sk, DH_A), lambda h, bi: (0, bi, 0, h)),
            pl.BlockSpec((1, 1, sk, DH_A), lambda h, bi: (1, bi, 0, h)),
            pl.BlockSpec((1, tq, wide), lambda h, bi: (h, 0, 0)),
        ],
        out_specs=pl.BlockSpec((1, sq, DH_A), lambda h, bi: (bi, 0, h)),
        out_shape=jax.ShapeDtypeStruct(q.shape, BF16),
        scratch_shapes=[pltpu.VMEM((sk, DH_A), BF16), pltpu.VMEM((sk, DH_A), BF16),
                        pltpu.VMEM((tq, w), F32), pltpu.VMEM((tq, w), F32)],
        compiler_params=_params(("arbitrary", "arbitrary"), vmem),
        name=name,
    )(q, kv, kv, g)


def _attn_a_step_kernel(q_ref, kh_ref, vh_ref, kvn_ref, g_ref, o_ref, *, hl):
    for h in range(H_A):
        cols = slice(h * DH_A, (h + 1) * DH_A)
        q = q_ref[:, cols]
        kh = kh_ref[pl.ds(h, hl, stride=H_A), :].astype(BF16)
        vh = vh_ref[pl.ds(h, hl, stride=H_A), :].astype(BF16)
        kn = kvn_ref[0, :, cols].astype(BF16)
        vn = kvn_ref[1, :, cols].astype(BF16)
        g = g_ref[h]
        nt = (((1,), (1,)), ((), ()))
        s_h = lax.dot_general(q, kh, nt, preferred_element_type=F32) + g[:, :hl]
        s_n = lax.dot_general(q, kn, nt, preferred_element_type=F32) + g[:, hl:]
        m = jnp.maximum(jnp.max(s_h, axis=-1, keepdims=True), jnp.max(s_n, axis=-1, keepdims=True))
        e_h = jnp.exp2(s_h - m)
        e_n = jnp.exp2(s_n - m)
        l = jnp.sum(e_h, axis=-1, keepdims=True) + jnp.sum(e_n, axis=-1, keepdims=True)
        o = (jnp.dot(e_h.astype(BF16), vh, preferred_element_type=F32)
             + jnp.dot(e_n.astype(BF16), vn, preferred_element_type=F32))
        o_ref[:, cols] = (o * (1.0 / l)).astype(BF16)


def _attn_a_step_call(q2d, hist_k, hist_v, kv_new, rel_table, *, bn, name):
    t, d = q2d.shape
    s = t // bn
    hl = hist_k.shape[1]
    band = N_PREV_A * CHUNK
    assert hl == band and s <= CHUNK and PAST_LEN % CHUNK == 0
    g = _bias_wide_a(rel_table, tq=s, w=hl + s, q_offset=hl)
    hist_spec = pl.BlockSpec((None, hl * H_A, DH_A), lambda b: (b, 0, 0))
    vmem = 2 * (2 * hl * H_A * DH_A * 4 + 2 * s * d * 2 + 2 * s * d * 4) + 4 * H_A * s * (hl + s) * 4 + 8 * MIB
    return pl.pallas_call(
        functools.partial(_attn_a_step_kernel, hl=hl),
        grid=(bn,),
        in_specs=[
            pl.BlockSpec((s, d), lambda b: (b, 0)),
            hist_spec, hist_spec,
            pl.BlockSpec((2, s, d), lambda b: (0, b, 0)),
            pl.BlockSpec(g.shape, lambda b: (0, 0, 0)),
        ],
        out_specs=pl.BlockSpec((s, d), lambda b: (b, 0)),
        out_shape=jax.ShapeDtypeStruct((t, d), BF16),
        compiler_params=_params(("arbitrary",), vmem),
        name=name,
    )(q2d, hist_k.reshape(bn, hl * H_A, DH_A), hist_v.reshape(bn, hl * H_A, DH_A), kv_new, g)


def _attn_b_kernel(sink_ref, q_ref, k_ref, v_ref, mask_ref, o_ref, kd_scr, vd_scr, s0_scr, s1_scr,
                   *, tq, w, nqb, q_offset, nvar):
    j = pl.program_id(0)
    sk = kd_scr.shape[0]
    rows = min(CAST_ROWS, sk)
    assert sk % rows == 0
    n_blk = G_B // 2

    own_half = (lax.broadcasted_iota(jnp.int32, (rows, LANES), 1) >= DH_B).astype(jnp.int32) == (j & 1)
    for r0 in range(0, sk, rows):
        for src, dst in ((k_ref, kd_scr), (v_ref, vd_scr)):
            x = src[0, 0, r0:r0 + rows, :]
            dst[r0:r0 + rows, :] = jnp.where(own_half, x, pltpu.roll(x, DH_B, 1)).astype(BF16)

    lane_lo = lax.broadcasted_iota(jnp.int32, (tq, LANES), 1) < DH_B
    row = lax.broadcasted_iota(jnp.int32, (G_B * tq, 1), 0)
    skcol = jnp.full((G_B * tq, 1), sink_ref[G_B * j + G_B - 1], F32)
    for g in range(G_B - 2, -1, -1):
        skcol = jnp.where(row < (g + 1) * tq, sink_ref[G_B * j + g], skcol)
    skcol = skcol * LOG2E

    s_scr = (s0_scr, s1_scr)
    starts = functools.partial(_block_starts, tq=tq, q_offset=q_offset, n_prev=N_PREV_B, nvar=nvar)

    def scores(qi, slot):
        qstart, kstart, var = starts(qi)
        kw = kd_scr[pl.ds(kstart, w), :]
        parts = []
        for c in range(n_blk):
            qc = q_ref[0, pl.ds(qstart, tq), c * LANES:(c + 1) * LANES]
            zero = jnp.zeros_like(qc)
            parts += [jnp.where(lane_lo, qc, zero), jnp.where(lane_lo, zero, qc)]
        lhs = jnp.concatenate(parts, axis=0)
        s = lax.dot_general(lhs, kw, (((1,), (1,)), ((), ())), preferred_element_type=F32)
        s_scr[slot][...] = (s.reshape(G_B, tq, w) + mask_ref[var][None]).reshape(G_B * tq, w)

    def finish(qi, slot):
        qstart, kstart, _ = starts(qi)
        s = s_scr[slot][...]
        m = jnp.maximum(jnp.max(s, axis=-1, keepdims=True), skcol)
        e = jnp.exp2(s - m)
        den = jnp.sum(e, axis=-1, keepdims=True) + jnp.exp2(skcol - m)
        o2 = jnp.dot(e.astype(BF16), vd_scr[pl.ds(kstart, w), :], preferred_element_type=F32)
        o2 = o2 * (1.0 / den)
        for c in range(n_blk):
            lo = o2[2 * c * tq:(2 * c + 1) * tq]
            hi = o2[(2 * c + 1) * tq:(2 * c + 2) * tq]
            o_ref[0, pl.ds(qstart, tq), c * LANES:(c + 1) * LANES] = (
                jnp.where(lane_lo, lo, hi).astype(BF16))

    _pipelined_blocks(nqb, scores, finish)


def _attn_b_call(q, kv, sink, *, tq, q_offset, n_valid, name):
    b, sq, _ = q.shape
    sk = kv.shape[2]
    band = N_PREV_B * CHUNK
    w = band + tq
    nqb = sq // tq
    nvar = _num_variants(tq=tq, q_offset=q_offset, n_prev=N_PREV_B)
    qw = G_B * DH_B
    assert nqb == 1 or n_valid == sk
    assert (nqb - 1) * tq + q_offset - band + w <= sk
    masks = np.stack([
        np.where(_band_valid(tq=tq, w=w, qoff=min(q_offset + v * tq, band), n_prev=N_PREV_B,
                             n_valid=n_valid), 0.0, NEG_INF) for v in range(nvar)]).astype(np.float32)
    vmem = (2 * (2 * sq * qw * 2 + 2 * sk * LANES * 4 + nvar * tq * w * 4) + 2 * sk * LANES * 2
            + 12 * G_B * tq * w * 4 + 4 * MIB)
    return pl.pallas_call(
        functools.partial(_attn_b_kernel, tq=tq, w=w, nqb=nqb, q_offset=q_offset, nvar=nvar),
        grid=(KV_B, b),
        in_specs=[
            pl.BlockSpec(memory_space=pltpu.SMEM),
            pl.BlockSpec((1, sq, qw), lambda j, bi: (bi, 0, j)),
            pl.BlockSpec((1, 1, sk, LANES), lambda j, bi: (0, bi, 0, j // 2)),
            pl.BlockSpec((1, 1, sk, LANES), lambda j, bi: (1, bi, 0, j // 2)),
            pl.BlockSpec((nvar, tq, w), lambda j, bi: (0, 0, 0)),
        ],
        out_specs=pl.BlockSpec((1, sq, qw), lambda j, bi: (bi, 0, j)),
        out_shape=jax.ShapeDtypeStruct(q.shape, BF16),
        scratch_shapes=[pltpu.VMEM((sk, LANES), BF16), pltpu.VMEM((sk, LANES), BF16),
                        pltpu.VMEM((G_B * tq, w), F32), pltpu.VMEM((G_B * tq, w), F32)],
        compiler_params=_params(("arbitrary", "arbitrary"), vmem),
        name=name,
    )(sink, q, kv, kv, jnp.asarray(masks))


def _mod_rows(mod_ref, rows):
    return mod_ref[0] if mod_ref.shape[1] == 1 else mod_ref[0, rows]


def _wo_ln_kernel(o_ref, x_ref, gate_ref, w_ref, g_ref, b_ref, out_ref):
    tm = o_ref.shape[0]
    rc = min(EPILOGUE_ROWS, tm)
    for r0 in range(0, tm, rc):
        rows = slice(r0, r0 + rc)
        y = jnp.dot(o_ref[rows], w_ref[...], preferred_element_type=F32)
        z = ALPHA * x_ref[rows] + _mod_rows(gate_ref, rows) * y
        out_ref[rows] = _layer_norm(z, g_ref[...], b_ref[...])


def _wo_ln_call(o2d, x2d, gate, w, ln_g, ln_b, *, tm, tiles_per_group, name):
    t, d = x2d.shape
    r = gate.shape[1]
    vmem = 2 * (tm * d * 2 + 2 * tm * d * 4 + d * d * 2) + 4 * tm * d * 4 + 4 * MIB
    return pl.pallas_call(
        _wo_ln_kernel,
        grid=(t // tm,),
        in_specs=[
            pl.BlockSpec((tm, d), lambda i: (i, 0)),
            pl.BlockSpec((tm, d), lambda i: (i, 0)),
            pl.BlockSpec((1, r, d), lambda i: (i // tiles_per_group, 0, 0)),
            pl.BlockSpec((d, d), lambda i: (0, 0)),
            pl.BlockSpec((1, d), lambda i: (0, 0)),
            pl.BlockSpec((1, d), lambda i: (0, 0)),
        ],
        out_specs=pl.BlockSpec((tm, d), lambda i: (i, 0)),
        out_shape=jax.ShapeDtypeStruct((t, d), F32),
        compiler_params=_params(("arbitrary",), vmem),
        name=name,
    )(o2d, x2d, gate, w, ln_g, ln_b)


def _ffn_kernel(x_ref, shift_ref, scale_ref, gate_ref, wu_ref, wd_ref, g_ref, b_ref,
                out_ref, h_scr):
    j = pl.program_id(1)

    @pl.when(j == 0)
    def _():
        h = x_ref[...] * (1.0 + scale_ref[0]) + shift_ref[0]
        h_scr[...] = h.astype(BF16)
        out_ref[...] = jnp.zeros_like(out_ref)

    tm, d = out_ref.shape

    def mlp(rows):
        u = jnp.maximum(jnp.dot(h_scr[rows], wu_ref[...], preferred_element_type=F32), 0.0)
        ub = (u * u).astype(BF16)
        for c in range(d // FFN_DOWN_COLS):
            cols = slice(c * FFN_DOWN_COLS, (c + 1) * FFN_DOWN_COLS)
            out_ref[rows, cols] += jnp.dot(ub, wd_ref[:, cols], preferred_element_type=F32)

    last = pl.num_programs(1) - 1

    @pl.when(j < last)
    def _():
        mlp(slice(0, tm))

    @pl.when(j == last)
    def _():
        rc = min(FFN_LAST_ROWS, tm)
        for r0 in range(0, tm, rc):
            rows = slice(r0, r0 + rc)
            mlp(rows)
            z = ALPHA * x_ref[rows] + _mod_rows(gate_ref, rows) * out_ref[rows]
            out_ref[rows] = _layer_norm(z, g_ref[...], b_ref[...])


def _ffn_call(x2d, shift, scale, gate, w_up, w_down, ln_g, ln_b, *, tm, tf, tiles_per_group, name):
    t, d = x2d.shape
    dff = w_up.shape[1]
    r = gate.shape[1]
    mod_spec = pl.BlockSpec((1, r, d), lambda i, j: (i // tiles_per_group, 0, 0))
    vmem = 4 * tm * d * 4 + tm * d * 2 + 4 * d * tf * 2 + 3 * tm * tf * 4 + 2 * tm * d * 4 + 4 * MIB
    return pl.pallas_call(
        _ffn_kernel,
        grid=(t // tm, dff // tf),
        in_specs=[
            pl.BlockSpec((tm, d), lambda i, j: (i, 0)),
            mod_spec, mod_spec, mod_spec,
            pl.BlockSpec((d, tf), lambda i, j: (0, j)),
            pl.BlockSpec((tf, d), lambda i, j: (j, 0)),
            pl.BlockSpec((1, d), lambda i, j: (0, 0)),
            pl.BlockSpec((1, d), lambda i, j: (0, 0)),
        ],
        out_specs=pl.BlockSpec((tm, d), lambda i, j: (i, 0)),
        out_shape=jax.ShapeDtypeStruct((t, d), F32),
        scratch_shapes=[pltpu.VMEM((tm, d), BF16)],
        compiler_params=_params(("arbitrary", "arbitrary"), vmem),
        name=name,
    )(x2d, shift, scale, gate, w_up, w_down, ln_g, ln_b)


def _rope_tables(pos):
    half = ROT_DIM // 2
    freqs = ROPE_THETA ** (-jnp.arange(0, ROT_DIM, 2, dtype=F32) / ROT_DIM)
    ang = pos.astype(F32)[:, None] * freqs[None, :]
    cos, sin = jnp.cos(ang), jnp.sin(ang)
    n = pos.shape[0]
    ones = jnp.ones((n, DH_B - ROT_DIM), F32)
    zeros = jnp.zeros((n, DH_B - ROT_DIM), F32)
    z8 = jnp.zeros((n, half), F32)
    cos_h = jnp.concatenate([cos, cos, ones], axis=1)
    sup_h = jnp.concatenate([-sin, z8, zeros], axis=1)
    sdn_h = jnp.concatenate([z8, sin, zeros], axis=1)
    return tuple(jnp.concatenate([t, t], axis=1) for t in (cos_h, sup_h, sdn_h))


def _trunk(x, mod, pos0, hist, wts, *, tm, tf, tq_a, tq_b):
    bn, s, d = x.shape
    t = bn * s
    per_row = tm > s
    tm_wo = min(tm, 512)
    if per_row:
        assert t == tm
        tiles_per_group = 1

        def mod3(m):
            return jnp.repeat(m, s, axis=0).reshape(1, t, d)
    else:
        tiles_per_group = s // tm

        def mod3(m):
            return m.reshape(bn, 1, d)

    def split_mod(p):
        m = mod[p]
        return mod3(m[:, :d]), mod3(m[:, d:2 * d]), mod3(m[:, 2 * d:])

    x2d = x.reshape(t, d)
    pos = pos0 + jnp.arange(s)
    rope_tabs = _rope_tables(pos)
    if per_row:
        rope_tabs = tuple(jnp.tile(tb, (bn, 1)) for tb in rope_tabs)

    def pad_rows(a, n):
        return jnp.pad(a, ((0, 0), (0, n - a.shape[1]), (0, 0)))

    shift, scale, gate = split_mod(0)
    q, kv = _qkv_call(x2d, shift, scale, wts["w_qkv_a"], q_cols=d, kv_cols=d,
                      q_scale=DH_A ** -0.5 * LOG2E, tm=tm, tn=min(1024, d),
                      tiles_per_group=tiles_per_group, name="qkv_a")
    new_a = kv.reshape(2, bn, s, d)[:, :, -N_PREV_A * CHUNK:].reshape(2, 1, bn, -1, H_A, DH_A)
    if hist is None:
        o = _attn_a_call(q.reshape(bn, s, d), kv.reshape(2, bn, s, d), wts["rel_bias_a"],
                         tq=tq_a, name="attn_a")
    else:
        o = _attn_a_step_call(q, hist["a_k"], hist["a_v"], kv, wts["rel_bias_a"], bn=bn,
                              name="attn_a")
    x2d = _wo_ln_call(o.reshape(t, d), x2d, gate, wts["w_o_a"], wts["ln_g"][0, 0], wts["ln_b"][0, 0],
                      tm=tm_wo, tiles_per_group=tiles_per_group * (tm // tm_wo), name="wo_ln_a")
    shift, scale, gate = split_mod(1)
    x2d = _ffn_call(x2d, shift, scale, gate, wts["w_up"][0], wts["w_down"][0],
                    wts["ln_g"][0, 1], wts["ln_b"][0, 1], tm=tm, tf=tf,
                    tiles_per_group=tiles_per_group, name="ffn_0")

    shift, scale, gate = split_mod(2)
    kvw = KV_B * DH_B
    q, kv = _qkv_rope_call(x2d, shift, scale, wts["w_qkv_b"], rope_tabs, q_cols=d, kv_cols=kvw,
                           q_scale=DH_B ** -0.5 * LOG2E, tm=tm_wo,
                           tiles_per_group=tiles_per_group * (tm // tm_wo), name="qkv_b")
    kv = kv.reshape(2, bn, s, kvw)
    new_b = kv[:, :, -N_PREV_B * CHUNK:].reshape(2, 1, bn, -1, KV_B, DH_B)
    q = q.reshape(bn, s, d)
    if hist is None:
        o = _attn_b_call(q, kv, wts["sink_b"], tq=tq_b, q_offset=0, n_valid=s, name="attn_b")
    else:
        hk = hist["b_k"].reshape(bn, -1, kvw)
        hv = hist["b_v"].reshape(bn, -1, kvw)
        hl = hk.shape[1]
        sk = N_PREV_B * CHUNK + tq_b
        kv_ext = jnp.stack([pad_rows(jnp.concatenate([hk, kv[0]], axis=1), sk),
                            pad_rows(jnp.concatenate([hv, kv[1]], axis=1), sk)])
        o = _attn_b_call(pad_rows(q, tq_b), kv_ext, wts["sink_b"], tq=tq_b,
                         q_offset=hl, n_valid=hl + s, name="attn_b")[:, :s]
    x2d = _wo_ln_call(o.reshape(t, d), x2d, gate, wts["w_o_b"], wts["ln_g"][1, 0], wts["ln_b"][1, 0],
                      tm=tm_wo, tiles_per_group=tiles_per_group * (tm // tm_wo), name="wo_ln_b")
    shift, scale, gate = split_mod(3)
    x2d = _ffn_call(x2d, shift, scale, gate, wts["w_up"][1], wts["w_down"][1],
                    wts["ln_g"][1, 1], wts["ln_b"][1, 1], tm=tm, tf=tf,
                    tiles_per_group=tiles_per_group, name="ffn_1")
    return x2d.reshape(bn, s, d), new_a[0], new_a[1], new_b[0], new_b[1]


def kernel(x_prompt, x_sample, cache_a_k, cache_a_v, cache_b_k, cache_b_v, c_prompt, c_sample,
           w_ada, b_ada, ln_g, ln_b, w_qkv_a, w_o_a, rel_bias_a, w_qkv_b, w_o_b, sink_b, w_up, w_down):
    d = D_MODEL
    bp = c_prompt.shape[0]
    c_all = jnp.concatenate([c_prompt, c_sample], axis=0)
    mod = _ada_call(c_all, w_ada.reshape(DEPTH * 2, d, 3 * d), b_ada.reshape(DEPTH * 2, 1, 3 * d))

    wts = {
        "w_qkv_a": w_qkv_a[0].astype(BF16),
        "w_o_a": w_o_a[0].astype(BF16),
        "w_qkv_b": w_qkv_b[0].astype(BF16),
        "w_o_b": w_o_b[0].astype(BF16),
        "w_up": [w_up[i].astype(BF16) for i in range(DEPTH)],
        "w_down": [w_down[i].astype(BF16) for i in range(DEPTH)],
        "rel_bias_a": rel_bias_a[0],
        "sink_b": sink_b[0],
        "ln_g": ln_g.reshape(DEPTH, 2, 1, d),
        "ln_b": ln_b.reshape(DEPTH, 2, 1, d),
    }

    y_p, ak_p, av_p, bk_p, bv_p = _trunk(
        x_prompt, mod[:, :bp], 0, None, wts, tm=1024, tf=512, tq_a=256, tq_b=128)
    hist = {"a_k": cache_a_k[0], "a_v": cache_a_v[0], "b_k": cache_b_k[0], "b_v": cache_b_v[0]}
    n_s = x_sample.shape[0] * x_sample.shape[1]
    y_s, ak_s, av_s, bk_s, bv_s = _trunk(
        x_sample, mod[:, bp:], PAST_LEN, hist, wts, tm=n_s, tf=512, tq_a=CHUNK, tq_b=2 * CHUNK)
    return (y_p, y_s, ak_p, av_p, bk_p, bv_p, ak_s, av_s, bk_s, bv_s)
```
